```python
import jax
import jax.numpy as jnp
from jax import lax
import numpy as np

D_MODEL = 1024
BATCH = 8
SEQ = 4096
DEPTH = 2

DN_HEADS = 4
DN_HEAD_DIM = 128
DN_WIDTH = DN_HEADS * DN_HEAD_DIM
DN_CONV = 4
DN_CHUNK = 64
MB_HEADS = 8
MB_HEAD_DIM = 64
MB_WIDTH = MB_HEADS * MB_HEAD_DIM
MB_BLOCK = 256
MB_TOPK = 3
MB_QCHUNK = 32
N_BRANCH = 2
IN_COLS = 4 * DN_WIDTH + 2 * DN_HEADS + 4 * MB_WIDTH + N_BRANCH * D_MODEL
NORM_EPS = 1e-6

kernel_name = 'hybrid_deltanet_moba_gated_block'


def rms_norm(x, gain):
    xf = x.astype(jnp.float32)
    y = xf * lax.rsqrt(jnp.mean(xf * xf, axis=-1, keepdims=True) + NORM_EPS)
    return (y * gain.astype(jnp.float32)).astype(x.dtype)


def l2_normalize(x):
    xf = x.astype(jnp.float32)
    return (xf * lax.rsqrt(jnp.sum(xf * xf, axis=-1, keepdims=True) + NORM_EPS)).astype(x.dtype)


def causal_depthwise_conv(x, w):
    k_width, channels = w.shape
    return lax.conv_general_dilated(
        x, w[:, None, :].astype(x.dtype), window_strides=(1,),
        padding=((k_width - 1, 0),), dimension_numbers=('NWC', 'WIO', 'NWC'),
        feature_group_count=channels)


def to_heads(t, n_heads):
    b, s, _ = t.shape
    return t.reshape(b, s, n_heads, -1).transpose(0, 2, 1, 3)


def gated_delta_rule_chunked(q, k, v, g, beta):
    out_dtype = v.dtype
    f32 = jnp.float32
    b, h, s, dk = k.shape
    dv = v.shape[-1]
    c = DN_CHUNK
    n = s // c
    beta = beta.astype(f32)[..., None]
    qf = (q.astype(f32) * (dk ** -0.5)).reshape(b, h, n, c, dk)
    kf = k.astype(f32)
    vf = v.astype(f32)
    k_beta = (kf * beta).reshape(b, h, n, c, dk)
    v_beta = (vf * beta).reshape(b, h, n, c, dv)
    kf = kf.reshape(b, h, n, c, dk)
    g = jnp.cumsum(g.astype(f32).reshape(b, h, n, c), axis=-1)
    causal = jnp.tril(jnp.ones((c, c), dtype=bool))
    strict = jnp.tril(jnp.ones((c, c), dtype=bool), k=-1)
    decay = jnp.exp(jnp.where(causal, g[..., :, None] - g[..., None, :], -jnp.inf))
    a = jnp.where(strict, jnp.einsum('bhnid,bhnjd->bhnij', k_beta, kf) * decay, 0.0)
    eye = jnp.eye(c, dtype=f32)
    t_inv = lax.linalg.triangular_solve(eye + a, jnp.broadcast_to(eye, a.shape),
                                        left_side=True, lower=True)
    u = t_inv @ v_beta
    w = t_inv @ (k_beta * jnp.exp(g)[..., None])
    qk = jnp.einsum('bhnid,bhnjd->bhnij', qf, kf) * decay

    def chunk_step(state, xs):
        q_i, k_i, u_i, w_i, qk_i, g_i = xs
        v_new = u_i - w_i @ state
        o_i = (q_i * jnp.exp(g_i)[..., None]) @ state + qk_i @ v_new
        g_last = g_i[..., -1:]
        state = state * jnp.exp(g_last)[..., None] + jnp.einsum(
            'bhcd,bhce->bhde', k_i * jnp.exp(g_last - g_i)[..., None], v_new)
        return state, o_i

    xs = tuple(jnp.moveaxis(t_, 2, 0) for t_ in (qf, kf, u, w, qk, g))
    state0 = jnp.zeros((b, h, dk, dv), f32)
    _, o = lax.scan(chunk_step, state0, xs)
    return jnp.moveaxis(o, 0, 2).reshape(b, h, s, dv).astype(out_dtype)


def moba_attention(q, k, v):
    b, h, s, dh = q.shape
    nb = s // MB_BLOCK
    n_sel = min(MB_TOPK, nb)
    scale = dh ** -0.5
    kb = k.reshape(b, h, nb, MB_BLOCK, dh)
    vb = v.reshape(b, h, nb, MB_BLOCK, dh)
    k_mean = jnp.mean(kb.astype(jnp.float32), axis=3).astype(k.dtype)
    bi = jnp.arange(b)[:, None, None, None]
    hi = jnp.arange(h)[None, :, None, None]
    blk_ids = jnp.arange(nb)

    def attend_chunk(ci):
        start = ci * MB_QCHUNK
        own = start // MB_BLOCK
        q_c = lax.dynamic_slice_in_dim(q, start, MB_QCHUNK, axis=2)
        q_pos = start + jnp.arange(MB_QCHUNK)
        gate = jnp.einsum('bhqd,bhnd->bhqn', q_c, k_mean).astype(jnp.float32)
        gate = jnp.where(blk_ids < own, gate, -jnp.inf)
        _, sel = lax.top_k(gate, n_sel)
        sel_ok = sel < own
        k_sel = kb[bi, hi, sel]
        v_sel = vb[bi, hi, sel]
        s_sel = jnp.einsum('bhqd,bhqjtd->bhqjt', q_c, k_sel).astype(jnp.float32) * scale
        s_sel = jnp.where(sel_ok[..., None], s_sel, -jnp.inf).reshape(b, h, MB_QCHUNK, n_sel * MB_BLOCK)
        k_own = lax.dynamic_slice_in_dim(k, own * MB_BLOCK, MB_BLOCK, axis=2)
        v_own = lax.dynamic_slice_in_dim(v, own * MB_BLOCK, MB_BLOCK, axis=2)
        k_pos = own * MB_BLOCK + jnp.arange(MB_BLOCK)
        s_own = jnp.einsum('bhqd,bhtd->bhqt', q_c, k_own).astype(jnp.float32) * scale
        s_own = jnp.where(k_pos[None, :] <= q_pos[:, None], s_own, -jnp.inf)
        p = jax.nn.softmax(jnp.concatenate([s_own, s_sel], axis=-1), axis=-1).astype(v.dtype)
        p_own = p[..., :MB_BLOCK]
        p_sel = p[..., MB_BLOCK:].reshape(b, h, MB_QCHUNK, n_sel, MB_BLOCK)
        return (jnp.einsum('bhqt,bhtd->bhqd', p_own, v_own)
                + jnp.einsum('bhqjt,bhqjtd->bhqd', p_sel, v_sel))

    out = lax.map(attend_chunk, jnp.arange(s // MB_QCHUNK))
    return jnp.moveaxis(out, 0, 2).reshape(b, h, s, dh)


def hybrid_layer(x, c, w_ada, b_ada, g_pre, g_post, w_in, conv_w, a_log, dt_bias,
                 dn_norm_g, w_proj_dn, w_proj_mb, w_out):
    b, s, _ = x.shape
    f32 = jnp.float32
    shift, scale, gate = jnp.split(jax.nn.silu(c) @ w_ada + b_ada, 3, axis=-1)
    h = rms_norm(x, g_pre) * (1 + scale[:, None, :]) + shift[:, None, :]
    sizes = (3 * DN_WIDTH, DN_WIDTH, DN_HEADS, DN_HEADS, 3 * MB_WIDTH, MB_WIDTH, N_BRANCH * D_MODEL)
    cuts = np.cumsum(sizes)[:-1].tolist()
    qkv_dn, z_dn, beta_logit, a_logit, qkv_mb, z_mb, merge_logit = jnp.split(h @ w_in, cuts, axis=-1)

    qkv_dn = jax.nn.silu(causal_depthwise_conv(qkv_dn, conv_w))
    q_dn, k_dn, v_dn = (to_heads(t_, DN_HEADS) for t_ in jnp.split(qkv_dn, 3, axis=-1))
    beta = jax.nn.sigmoid(beta_logit).transpose(0, 2, 1)
    g = (-jnp.exp(a_log.astype(f32))
         * jax.nn.softplus(a_logit.astype(f32) + dt_bias.astype(f32))).transpose(0, 2, 1)
    o_dn = gated_delta_rule_chunked(l2_normalize(q_dn), l2_normalize(k_dn), v_dn, g, beta)
    o_dn = rms_norm(o_dn.transpose(0, 2, 1, 3), dn_norm_g) * jax.nn.silu(
        z_dn.reshape(b, s, DN_HEADS, DN_HEAD_DIM))
    y_dn = o_dn.reshape(b, s, DN_WIDTH) @ w_proj_dn

    pad = (-s) % MB_BLOCK
    q_mb, k_mb, v_mb = (jnp.pad(to_heads(t_, MB_HEADS), ((0, 0), (0, 0), (0, pad), (0, 0)))
                        for t_ in jnp.split(qkv_mb, 3, axis=-1))
    o_mb = moba_attention(q_mb, k_mb, v_mb)[:, :, :s]
    o_mb = o_mb.transpose(0, 2, 1, 3).reshape(b, s, MB_WIDTH) * jax.nn.silu(z_mb)
    y_mb = o_mb @ w_proj_mb

    gate_dn, gate_mb = jnp.split(jax.nn.sigmoid(merge_logit), N_BRANCH, axis=-1)
    mixed = (gate_dn * y_dn + gate_mb * y_mb) @ w_out
    return x + gate[:, None, :] * rms_norm(mixed, g_post)


def setup_inputs(seed: int = 0) -> dict:
    key = jax.random.key(seed)
    ks = jax.random.split(key, 16)
    nrm = jax.random.normal
    d = D_MODEL
    dt = jnp.exp(jax.random.uniform(ks[9], (DEPTH, DN_HEADS), minval=np.log(1e-3), maxval=np.log(1e-1)))
    return {
        'x': nrm(ks[0], (BATCH, SEQ, d), jnp.float32),
        'c': nrm(ks[1], (BATCH, d), jnp.float32),
        'w_ada': nrm(ks[2], (DEPTH, d, 3 * d), jnp.float32) * (0.5 * d ** -0.5),
        'b_ada': nrm(ks[3], (DEPTH, 3 * d), jnp.float32) * 0.1,
        'g_pre': 1.0 + 0.02 * nrm(ks[4], (DEPTH, d), jnp.float32),
        'g_post': 1.0 + 0.02 * nrm(ks[5], (DEPTH, d), jnp.float32),
        'w_in': nrm(ks[6], (DEPTH, d, IN_COLS), jnp.float32) * d ** -0.5,
        'conv_w': nrm(ks[7], (DEPTH, DN_CONV, 3 * DN_WIDTH), jnp.float32) * DN_CONV ** -0.5,
        'a_log': jnp.log(jax.random.uniform(ks[8], (DEPTH, DN_HEADS), minval=1.0, maxval=16.0)),
        'dt_bias': dt + jnp.log(-jnp.expm1(-dt)),
        'dn_norm_g': 1.0 + 0.02 * nrm(ks[10], (DEPTH, DN_HEAD_DIM), jnp.float32),
        'w_proj_dn': nrm(ks[11], (DEPTH, DN_WIDTH, d), jnp.float32) * DN_WIDTH ** -0.5,
        'w_proj_mb': nrm(ks[12], (DEPTH, MB_WIDTH, d), jnp.float32) * MB_WIDTH ** -0.5,
        'w_out': nrm(ks[13], (DEPTH, d, d), jnp.float32) * d ** -0.5,
    }


def reference(x, c, w_ada, b_ada, g_pre, g_post, w_in, conv_w, a_log, dt_bias,
              dn_norm_g, w_proj_dn, w_proj_mb, w_out):
    for l in range(DEPTH):
        x = hybrid_layer(x, c, w_ada[l], b_ada[l], g_pre[l], g_post[l], w_in[l], conv_w[l],
                         a_log[l], dt_bias[l], dn_norm_g[l], w_proj_dn[l], w_proj_mb[l], w_out[l])
    return x
```

```python
import functools

import jax
import jax.numpy as jnp
from jax import lax
from jax.experimental import pallas as pl
from jax.experimental.pallas import tpu as pltpu

F32 = jnp.float32
BF16 = jnp.bfloat16

NORM_EPS = 1e-6
DN_HEADS = 4
DN_HEAD_DIM = 128
DN_WIDTH = DN_HEADS * DN_HEAD_DIM
DN_CONV = 4
DN_CHUNK = 64
MB_HEADS = 8
MB_HEAD_DIM = 64
MB_WIDTH = MB_HEADS * MB_HEAD_DIM
MB_BLOCK = 256
MB_TOPK = 3

V7X_VMEM_LIMIT_BYTES = 56 * 1024 * 1024
LANES = 128
CONV_HALO = 8

ROW_TILE = 512
DN_TILE = 256
MASK_VALUE = -1e30

NT_DIMS = (((1,), (1,)), ((), ()))


def _silu(v):
    return v * jax.nn.sigmoid(v)


def _dot(a, b):
    return jnp.dot(a, b, preferred_element_type=F32)


def _dot_nt(a, b):
    return lax.dot_general(a, b, NT_DIMS, preferred_element_type=F32)


def _split_bf16(v):
    hi = v.astype(BF16)
    lo = (v - hi.astype(F32)).astype(BF16)
    return hi, lo


def _mod_kernel(c_ref, w_ref, b_ref, o_ref):
    sc = _silu(c_ref[...])
    o_ref[0] = jnp.dot(sc, w_ref[0], preferred_element_type=F32,
                       precision=lax.Precision.HIGHEST) + b_ref[0]


def _modulation(c, w_ada, b_ada):
    depth, d, d3 = w_ada.shape
    b = c.shape[0]
    tn = 512
    return pl.pallas_call(
        _mod_kernel,
        grid=(depth, d3 // tn),
        in_specs=[
            pl.BlockSpec((b, d), lambda l, j: (0, 0)),
            pl.BlockSpec((1, d, tn), lambda l, j: (l, 0, j)),
            pl.BlockSpec((1, 1, tn), lambda l, j: (l, 0, j)),
        ],
        out_specs=pl.BlockSpec((1, b, tn), lambda l, j: (l, 0, j)),
        out_shape=jax.ShapeDtypeStruct((depth, b, d3), F32),
        name="adaln_modulation",
    )(c, w_ada, b_ada.reshape(depth, 1, d3))


_TOK_GROUPS = (
    ("qkv_dn", 3 * DN_WIDTH, BF16),
    ("z_dn", DN_WIDTH, BF16),
    ("qk_mb", 2 * MB_WIDTH, BF16),
    ("z_mb", MB_WIDTH, BF16),
    ("merge", None, BF16),
    ("ba", LANES, F32),
)


def _inproj_kernel(x_ref, mod_ref, gpre_ref, wtok_ref, wvt_ref, wbat_ref,
                   qkv_dn_ref, z_dn_ref, qk_mb_ref, z_mb_ref, merge_ref, ba_ref, vt_ref, bat_ref,
                   *, d_model):
    x = x_ref[0]
    shift = mod_ref[0, :, 0:d_model]
    scale = mod_ref[0, :, d_model:2 * d_model]
    xn = x * lax.rsqrt(jnp.mean(x * x, axis=-1, keepdims=True) + NORM_EPS)
    h = (xn * gpre_ref[...]) * (1.0 + scale) + shift
    hb = h.astype(BF16)

    col = 0
    for ref in (qkv_dn_ref, z_dn_ref, qk_mb_ref, z_mb_ref, merge_ref, ba_ref):
        width = ref.shape[-1]
        step = min(width, 512)
        for c0 in range(0, width, step):
            ref[0, :, c0:c0 + step] = _dot(hb, wtok_ref[:, col + c0:col + c0 + step]).astype(ref.dtype)
        col += width

    vt = _dot_nt(wvt_ref[...], hb).astype(BF16)
    for j in range(vt_ref.shape[1]):
        vt_ref[0, j] = vt[:, j * MB_BLOCK:(j + 1) * MB_BLOCK]
    bat_ref[0] = _dot_nt(wbat_ref[...], hb)


def _in_projection(x, mod, g_pre, w_tok, w_vt, w_bat):
    b, s, d = x.shape
    tm = ROW_TILE
    n_tok = w_tok.shape[1]
    kern = functools.partial(_inproj_kernel, d_model=d)
    const = lambda bi, ti: (0, 0)
    row = lambda bi, ti: (bi, ti, 0)
    widths = [3 * DN_WIDTH, DN_WIDTH, 2 * MB_WIDTH, MB_WIDTH, 2 * d, LANES]
    dtypes = [BF16, BF16, BF16, BF16, BF16, F32]
    out_shape = [jax.ShapeDtypeStruct((b, s, w), t) for w, t in zip(widths, dtypes)]
    out_specs = [pl.BlockSpec((1, tm, w), row) for w in widths]
    out_shape += [
        jax.ShapeDtypeStruct((b, s // MB_BLOCK, MB_WIDTH, MB_BLOCK), BF16),
        jax.ShapeDtypeStruct((b, 2 * DN_HEADS, s), F32),
    ]
    out_specs += [
        pl.BlockSpec((1, tm // MB_BLOCK, MB_WIDTH, MB_BLOCK), lambda bi, ti: (bi, ti, 0, 0)),
        pl.BlockSpec((1, 2 * DN_HEADS, tm), lambda bi, ti: (bi, 0, ti)),
    ]
    return pl.pallas_call(
        kern,
        grid=(b, s // tm),
        in_specs=[
            pl.BlockSpec((1, tm, d), row),
            pl.BlockSpec((1, 1, 3 * d), lambda bi, ti: (bi, 0, 0)),
            pl.BlockSpec((1, d), const),
            pl.BlockSpec((d, n_tok), const, pipeline_mode=pl.Buffered(1)),
            pl.BlockSpec((MB_WIDTH, d), const, pipeline_mode=pl.Buffered(1)),
            pl.BlockSpec((2 * DN_HEADS, d), const, pipeline_mode=pl.Buffered(1)),
        ],
        out_specs=out_specs,
        out_shape=out_shape,
        compiler_params=pltpu.CompilerParams(
            dimension_semantics=("parallel", "parallel"),
            vmem_limit_bytes=V7X_VMEM_LIMIT_BYTES),
        name="in_projection",
    )(x, mod, g_pre, w_tok, w_vt, w_bat)


def _dn_kernel(qkv_ref, ba_ref, bat_ref, z_ref, convw_ref, pcol_ref, prow_ref, gnorm_ref,
               o_ref, xs_ref, state_ref):
    t = pl.program_id(1)
    tt = qkv_ref.shape[1]
    n_chunks = tt // DN_CHUNK
    dk = DN_HEAD_DIM

    @pl.when(t == 0)
    def _():
        xs_ref[0:CONV_HALO, :] = jnp.zeros((CONV_HALO, xs_ref.shape[1]), F32)
        state_ref[...] = jnp.zeros(state_ref.shape, F32)

    xs_ref[CONV_HALO:CONV_HALO + tt, :] = qkv_ref[0].astype(F32)

    def conv_silu(c0):
        acc = None
        for k in range(DN_CONV):
            r0 = CONV_HALO - (DN_CONV - 1) + k
            term = convw_ref[k:k + 1, c0:c0 + dk] * xs_ref[r0:r0 + tt, c0:c0 + dk]
            acc = term if acc is None else acc + term
        return _silu(acc)

    ba = ba_ref[0]
    beta_all = jax.nn.sigmoid(ba)
    g_col = -jnp.exp(pcol_ref[0:1, :]) * jax.nn.softplus(ba + pcol_ref[1:2, :])
    g_row = -jnp.exp(prow_ref[:, 0:1]) * jax.nn.softplus(bat_ref[0] + prow_ref[:, 1:2])

    ri = lax.broadcasted_iota(jnp.int32, (tt, tt), 0)
    ci = lax.broadcasted_iota(jnp.int32, (tt, tt), 1)
    same_chunk = (ri // DN_CHUNK) == (ci // DN_CHUNK)
    causal = same_chunk & (ri >= ci)
    strict = same_chunk & (ri > ci)
    lower = causal.astype(BF16)
    upper = (same_chunk & (ri <= ci)).astype(BF16)
    gch, gcl = _split_bf16(g_col)
    gc_col = _dot(lower, gch) + _dot(lower, gcl)
    grh, grl = _split_bf16(g_row)
    gc_row = _dot(grh, upper) + _dot(grl, upper)

    eg = jnp.exp(gc_col)
    g_last = [gc_col[(c + 1) * DN_CHUNK - 1:(c + 1) * DN_CHUNK, :] for c in range(n_chunks)]
    eg_to_last = jnp.concatenate(
        [jnp.exp(g_last[c] - gc_col[c * DN_CHUNK:(c + 1) * DN_CHUNK, :]) for c in range(n_chunks)], axis=0)
    eg_last = [jnp.exp(g_last[c]) for c in range(n_chunks)]

    eye = (ri == ci).astype(F32)
    col_chunk = lax.broadcasted_iota(jnp.int32, (dk, tt), 1) // DN_CHUNK

    for h in range(DN_HEADS):
        q = conv_silu(h * dk)
        k = conv_silu(DN_WIDTH + h * dk)
        v = conv_silu(2 * DN_WIDTH + h * dk)
        qn = q * lax.rsqrt(jnp.sum(q * q, axis=-1, keepdims=True) + NORM_EPS) * (dk ** -0.5)
        kn = k * lax.rsqrt(jnp.sum(k * k, axis=-1, keepdims=True) + NORM_EPS)
        gl = DN_HEADS + h
        beta = beta_all[:, h:h + 1]
        eg_h = eg[:, gl:gl + 1]
        kb = kn * beta
        vb = v * beta
        kn_b = kn.astype(BF16)

        scores = _dot_nt(jnp.concatenate([kb.astype(BF16), qn.astype(BF16)], axis=0), kn_b)
        diff = gc_col[:, gl:gl + 1] - gc_row[gl:gl + 1, :]
        decay = jnp.where(causal, jnp.exp(jnp.where(causal, diff, 0.0)), 0.0)
        a = jnp.where(strict, scores[0:tt] * decay, 0.0)
        qk = (scores[tt:2 * tt] * decay).astype(BF16)

        p = eye - a
        xp = a
        n_sq = (DN_CHUNK - 1).bit_length() - 1
        for _ in range(n_sq):
            xb = xp.astype(BF16)
            xp = _dot(xb, xb)
            p = p + _dot(p.astype(BF16), xp.astype(BF16))
        uw = _dot(p.astype(BF16),
                  jnp.concatenate([vb.astype(BF16), (kb * eg_h).astype(BF16)], axis=1))
        u = uw[:, 0:dk]
        w_b = uw[:, dk:2 * dk].astype(BF16)
        qg_b = (qn * eg_h).astype(BF16)
        kdt = (kn * eg_to_last[:, gl:gl + 1]).T

        state = state_ref[h]
        v_rows = []
        o_rows = []
        for c in range(n_chunks):
            rows = slice(c * DN_CHUNK, (c + 1) * DN_CHUNK)
            r2 = _dot(jnp.concatenate([w_b[rows], qg_b[rows]], axis=0), state.astype(BF16))
            v_new = u[rows] - r2[0:DN_CHUNK]
            v_rows.append(v_new.astype(BF16))
            v_blk = jnp.concatenate(
                v_rows + [jnp.zeros(((n_chunks - 1 - c) * DN_CHUNK, dk), BF16)] * (c < n_chunks - 1), axis=0)
            kdt_c = jnp.where(col_chunk == c, kdt, 0.0).astype(BF16)
            r3 = _dot(jnp.concatenate([qk[rows], kdt_c], axis=0), v_blk)
            o_rows.append(r2[DN_CHUNK:2 * DN_CHUNK] + r3[0:DN_CHUNK])
            state = state * eg_last[c][:, gl:gl + 1] + r3[DN_CHUNK:DN_CHUNK + dk]
        state_ref[h] = state

        o = jnp.concatenate(o_rows, axis=0)
        on = o * lax.rsqrt(jnp.mean(o * o, axis=-1, keepdims=True) + NORM_EPS) * gnorm_ref[...]
        zh = z_ref[0, :, h * dk:(h + 1) * dk].astype(F32)
        o_ref[0, :, h * dk:(h + 1) * dk] = (on * _silu(zh)).astype(o_ref.dtype)

    xs_ref[0:CONV_HALO, :] = xs_ref[tt:tt + CONV_HALO, :]


def _deltanet(qkv, ba, bat, z, conv_w, p_col, p_row, g_norm):
    b, s, w3 = qkv.shape
    tt = DN_TILE
    row = lambda bi, ti: (bi, ti, 0)
    const = lambda bi, ti: (0, 0)
    return pl.pallas_call(
        _dn_kernel,
        grid=(b, s // tt),
        in_specs=[
            pl.BlockSpec((1, tt, w3), row),
            pl.BlockSpec((1, tt, LANES), row),
            pl.BlockSpec((1, 2 * DN_HEADS, tt), lambda bi, ti: (bi, 0, ti)),
            pl.BlockSpec((1, tt, DN_WIDTH), row),
            pl.BlockSpec((DN_CONV, w3), const),
            pl.BlockSpec((2, LANES), const),
            pl.BlockSpec((2 * DN_HEADS, 2), const),
            pl.BlockSpec((1, DN_HEAD_DIM), const),
        ],
        out_specs=pl.BlockSpec((1, tt, DN_WIDTH), row),
        out_shape=jax.ShapeDtypeStruct((b, s, DN_WIDTH), BF16),
        scratch_shapes=[
            pltpu.VMEM((CONV_HALO + tt, w3), F32),
            pltpu.VMEM((DN_HEADS, DN_HEAD_DIM, DN_HEAD_DIM), F32),
        ],
        compiler_params=pltpu.CompilerParams(
            dimension_semantics=("parallel", "arbitrary"),
            vmem_limit_bytes=V7X_VMEM_LIMIT_BYTES),
        name="deltanet",
    )(qkv, ba, bat, z, conv_w, p_col, p_row, g_norm)


def _moba_kernel(q_ref, k_ref, vt_ref, z_ref, o_ref, kmean_ref, sel_ref):
    i = pl.program_id(2)
    nb = vt_ref.shape[1]
    blk = MB_BLOCK
    dh = MB_HEAD_DIM
    heads = LANES // dh

    @pl.when(i == 0)
    def _():
        for j in range(nb):
            kj = k_ref[0, j * blk:(j + 1) * blk, :].astype(F32)
            kmean_ref[j:j + 1, :] = jnp.mean(kj, axis=0, keepdims=True)

    q = q_ref[0]
    lane = lax.broadcasted_iota(jnp.int32, (1, LANES), 1)
    jidx = lax.broadcasted_iota(jnp.int32, (nb, blk), 0)
    valid = jidx < i
    km = kmean_ref[...]

    q_heads = []
    for h in range(heads):
        in_head = (lane >= h * dh) & (lane < (h + 1) * dh)
        gate = lax.dot_general(jnp.where(in_head, km, 0.0), q.astype(F32), NT_DIMS,
                               preferred_element_type=F32, precision=lax.Precision.HIGHEST)
        gate = jnp.where(valid, gate, -jnp.inf)
        rank = jnp.zeros((nb, blk), jnp.int32)
        for jp in range(nb):
            other = gate[jp:jp + 1, :]
            beats = (other > gate) | ((other == gate) & (jp < jidx))
            rank = rank + beats.astype(jnp.int32)
        sel_ref[h] = ((rank < MB_TOPK) & valid).astype(F32)
        q_heads.append(jnp.where(in_head, q, jnp.zeros_like(q)) * (dh ** -0.5))

    start = pl.multiple_of(i * blk, blk)
    k_own = k_ref[0, pl.ds(start, blk), :]
    key_pos = lax.broadcasted_iota(jnp.int32, (blk, blk), 0)
    qry_pos = lax.broadcasted_iota(jnp.int32, (blk, blk), 1)
    carry = []
    for h in range(heads):
        st = jnp.where(key_pos <= qry_pos, _dot_nt(k_own, q_heads[h]), MASK_VALUE)
        m = jnp.max(st, axis=0, keepdims=True)
        p = jnp.exp(st - m)
        l = jnp.sum(p, axis=0, keepdims=True)
        acc = _dot(vt_ref[0, i, h * dh:(h + 1) * dh, :], p.astype(BF16))
        carry += [m, l, acc]

    def body(j, carry):
        kj = k_ref[0, pl.ds(pl.multiple_of(j * blk, blk), blk), :]
        out = []
        for h in range(heads):
            m, l, acc = carry[3 * h:3 * h + 3]
            keep = sel_ref[h, pl.ds(j, 1), :] > 0.5
            st = jnp.where(keep, _dot_nt(kj, q_heads[h]), MASK_VALUE)
            m_new = jnp.maximum(m, jnp.max(st, axis=0, keepdims=True))
            alpha = jnp.exp(m - m_new)
            p = jnp.exp(st - m_new)
            l = alpha * l + jnp.sum(p, axis=0, keepdims=True)
            acc = alpha * acc + _dot(vt_ref[0, j, h * dh:(h + 1) * dh, :], p.astype(BF16))
            out += [m_new, l, acc]
        return tuple(out)

    carry = lax.fori_loop(0, i, body, tuple(carry))
    ot = jnp.concatenate([carry[3 * h + 2] / carry[3 * h + 1] for h in range(heads)], axis=0)
    o = ot.T
    o_ref[0] = (o * _silu(z_ref[0].astype(F32))).astype(o_ref.dtype)


def _moba(qk, vt, z):
    b, s, _ = qk.shape
    nb = s // MB_BLOCK
    pairs = MB_WIDTH // LANES
    return pl.pallas_call(
        _moba_kernel,
        grid=(b, pairs, nb),
        in_specs=[
            pl.BlockSpec((1, MB_BLOCK, LANES), lambda bi, hp, i: (bi, i, hp)),
            pl.BlockSpec((1, s, LANES), lambda bi, hp, i: (bi, 0, pairs + hp)),
            pl.BlockSpec((1, nb, LANES, MB_BLOCK), lambda bi, hp, i: (bi, 0, hp, 0)),
            pl.BlockSpec((1, MB_BLOCK, LANES), lambda bi, hp, i: (bi, i, hp)),
        ],
        out_specs=pl.BlockSpec((1, MB_BLOCK, LANES), lambda bi, hp, i: (bi, i, hp)),
        out_shape=jax.ShapeDtypeStruct((b, s, MB_WIDTH), BF16),
        scratch_shapes=[
            pltpu.VMEM((nb, LANES), F32),
            pltpu.VMEM((LANES // MB_HEAD_DIM, nb, MB_BLOCK), F32),
        ],
        compiler_params=pltpu.CompilerParams(
            dimension_semantics=("parallel", "parallel", "arbitrary"),
            vmem_limit_bytes=V7X_VMEM_LIMIT_BYTES),
        name="moba_attention",
    )(qk, qk, vt, z)


def _outproj_kernel(x_ref, odn_ref, omb_ref, merge_ref, mod_ref, gpost_ref,
                    wdn_ref, wmb_ref, wout_ref, o_ref, *, d_model):
    y_dn = _dot(odn_ref[0], wdn_ref[...])
    y_mb = _dot(omb_ref[0], wmb_ref[...])
    gates = jax.nn.sigmoid(merge_ref[0].astype(F32))
    mixed_in = gates[:, 0:d_model] * y_dn + gates[:, d_model:2 * d_model] * y_mb
    mixed = _dot(mixed_in.astype(BF16), wout_ref[...])
    y = mixed * lax.rsqrt(jnp.mean(mixed * mixed, axis=-1, keepdims=True) + NORM_EPS) * gpost_ref[...]
    gate = mod_ref[0, :, 2 * d_model:3 * d_model]
    o_ref[0] = x_ref[0] + gate * y


def _out_projection(x, o_dn, o_mb, merge, mod, g_post, w_dn, w_mb, w_out):
    b, s, d = x.shape
    tm = ROW_TILE
    row = lambda bi, ti: (bi, ti, 0)
    const = lambda bi, ti: (0, 0)
    kern = functools.partial(_outproj_kernel, d_model=d)
    return pl.pallas_call(
        kern,
        grid=(b, s // tm),
        in_specs=[
            pl.BlockSpec((1, tm, d), row),
            pl.BlockSpec((1, tm, DN_WIDTH), row),
            pl.BlockSpec((1, tm, MB_WIDTH), row),
            pl.BlockSpec((1, tm, 2 * d), row),
            pl.BlockSpec((1, 1, 3 * d), lambda bi, ti: (bi, 0, 0)),
            pl.BlockSpec((1, d), const),
            pl.BlockSpec((DN_WIDTH, d), const),
            pl.BlockSpec((MB_WIDTH, d), const),
            pl.BlockSpec((d, d), const),
        ],
        out_specs=pl.BlockSpec((1, tm, d), row),
        out_shape=jax.ShapeDtypeStruct((b, s, d), F32),
        compiler_params=pltpu.CompilerParams(
            dimension_semantics=("parallel", "parallel"),
            vmem_limit_bytes=V7X_VMEM_LIMIT_BYTES),
        name="out_projection",
    )(x, o_dn, o_mb, merge, mod, g_post, w_dn, w_mb, w_out)


def _split_w_in(w_in, d_model):
    sizes = (3 * DN_WIDTH, DN_WIDTH, DN_HEADS, DN_HEADS, 3 * MB_WIDTH, MB_WIDTH, 2 * d_model)
    cuts = [0]
    for n in sizes:
        cuts.append(cuts[-1] + n)
    w_qkv_dn, w_z_dn, w_beta, w_a, w_qkv_mb, w_z_mb, w_merge = (
        w_in[:, cuts[i]:cuts[i + 1]] for i in range(len(sizes)))
    w_ba = jnp.concatenate([w_beta, w_a], axis=1)
    w_ba_pad = jnp.pad(w_ba, ((0, 0), (0, LANES - 2 * DN_HEADS)))
    w_tok = jnp.concatenate(
        [w_qkv_dn, w_z_dn, w_qkv_mb[:, 0:2 * MB_WIDTH], w_z_mb, w_merge, w_ba_pad], axis=1).astype(BF16)
    w_vt = w_qkv_mb[:, 2 * MB_WIDTH:3 * MB_WIDTH].T.astype(BF16)
    w_bat = w_ba.T.astype(BF16)
    return w_tok, w_vt, w_bat


def _layer(x, mod, g_pre, g_post, w_in, conv_w, a_log, dt_bias, dn_norm_g, w_proj_dn, w_proj_mb, w_out):
    b, s, d = x.shape
    w_tok, w_vt, w_bat = _split_w_in(w_in, d)
    qkv_dn, z_dn, qk_mb, z_mb, merge, ba, vt, bat = _in_projection(
        x, mod, g_pre.reshape(1, d), w_tok, w_vt, w_bat)

    zeros_h = jnp.zeros((DN_HEADS,), F32)
    a_vec = jnp.concatenate([zeros_h, a_log.astype(F32)])
    dt_vec = jnp.concatenate([zeros_h, dt_bias.astype(F32)])
    p_col = jnp.pad(jnp.stack([a_vec, dt_vec]), ((0, 0), (0, LANES - 2 * DN_HEADS)))
    p_row = jnp.stack([a_vec, dt_vec], axis=1)
    o_dn = _deltanet(qkv_dn, ba, bat, z_dn, conv_w.astype(F32), p_col, p_row,
                     dn_norm_g.reshape(1, DN_HEAD_DIM).astype(F32))
    o_mb = _moba(qk_mb, vt, z_mb)
    return _out_projection(x, o_dn, o_mb, merge, mod, g_post.reshape(1, d),
                           w_proj_dn.astype(BF16), w_proj_mb.astype(BF16), w_out.astype(BF16))


def kernel(x, c, w_ada, b_ada, g_pre, g_post, w_in, conv_w, a_log, dt_bias, dn_norm_g, w_proj_dn, w_proj_mb, w_out):
    depth = w_ada.shape[0]
    b, s, d = x.shape
    assert s % ROW_TILE == 0 and s % MB_BLOCK == 0 and s % DN_TILE == 0
    mod_all = _modulation(c, w_ada, b_ada)
    for l in range(depth):
        mod = mod_all[l].reshape(b, 1, 3 * d)
        x = _layer(x, mod, g_pre[l], g_post[l], w_in[l], conv_w[l], a_log[l], dt_bias[l],
                   dn_norm_g[l], w_proj_dn[l], w_proj_mb[l], w_out[l])
    return x
```

```python
import functools

import jax
import jax.numpy as jnp
from jax import lax
from jax.experimental import pallas as pl
from jax.experimental.pallas import tpu as pltpu

F32 = jnp.float32
BF16 = jnp.bfloat16

NORM_EPS = 1e-6
DN_HEADS = 4
DN_HEAD_DIM = 128
DN_WIDTH = DN_HEADS * DN_HEAD_DIM
DN_CONV = 4
DN_CHUNK = 64
MB_HEADS = 8
MB_HEAD_DIM = 64
MB_WIDTH = MB_HEADS * MB_HEAD_DIM
MB_BLOCK = 256
MB_TOPK = 3

V7X_VMEM_LIMIT_BYTES = 56 * 1024 * 1024
LANES = 128
CONV_HALO = 8

ROW_TILE = 512
DN_TILE = 256
KV_GROUP = 2
BF16_SUBLANES = 16
MASK_VALUE = -1e30
LOG2_E = 1.4426950408889634

NT_DIMS = (((1,), (1,)), ((), ()))


def _silu(v):
    return v * jax.nn.sigmoid(v)


def _dot(a, b):
    return jnp.dot(a, b, preferred_element_type=F32)


def _dot_nt(a, b):
    return lax.dot_general(a, b, NT_DIMS, preferred_element_type=F32)


def _split_bf16(v):
    hi = v.astype(BF16)
    lo = (v - hi.astype(F32)).astype(BF16)
    return hi, lo


def _mod_kernel(c_ref, w_ref, b_ref, o_ref):
    sc = _silu(c_ref[...])
    o_ref[0] = jnp.dot(sc, w_ref[0], preferred_element_type=F32,
                       precision=lax.Precision.HIGHEST) + b_ref[0]


def _modulation(c, w_ada, b_ada):
    depth, d, d3 = w_ada.shape
    b = c.shape[0]
    tn = 512
    return pl.pallas_call(
        _mod_kernel,
        grid=(depth, d3 // tn),
        in_specs=[
            pl.BlockSpec((b, d), lambda l, j: (0, 0)),
            pl.BlockSpec((1, d, tn), lambda l, j: (l, 0, j)),
            pl.BlockSpec((1, 1, tn), lambda l, j: (l, 0, j)),
        ],
        out_specs=pl.BlockSpec((1, b, tn), lambda l, j: (l, 0, j)),
        out_shape=jax.ShapeDtypeStruct((depth, b, d3), F32),
        name="adaln_modulation",
    )(c, w_ada, b_ada.reshape(depth, 1, d3))


def _inproj_kernel(x_ref, mod_ref, gpre_ref, wtok_ref, wvt_ref, wbat_ref,
                   qkv_dn_ref, z_dn_ref, qk_mb_ref, z_mb_ref, merge_ref, ba_ref, vt_ref, bat_ref,
                   *, d_model):
    x = x_ref[0]
    shift = mod_ref[0, :, 0:d_model]
    scale = mod_ref[0, :, d_model:2 * d_model]
    xn = x * lax.rsqrt(jnp.mean(x * x, axis=-1, keepdims=True) + NORM_EPS)
    h = (xn * gpre_ref[...]) * (1.0 + scale) + shift
    hb = h.astype(BF16)

    col = 0
    for ref in (qkv_dn_ref, z_dn_ref, qk_mb_ref, z_mb_ref, merge_ref, ba_ref):
        width = ref.shape[-1]
        step = min(width, 512)
        for c0 in range(0, width, step):
            ref[0, :, c0:c0 + step] = _dot(hb, wtok_ref[:, col + c0:col + c0 + step]).astype(ref.dtype)
        col += width

    vt = _dot_nt(wvt_ref[...], hb).astype(BF16)
    for j in range(vt_ref.shape[1]):
        vt_ref[0, j] = vt[:, j * MB_BLOCK:(j + 1) * MB_BLOCK]
    bat_ref[0] = _dot_nt(wbat_ref[...], hb)


def _in_projection(x, mod, g_pre, w_tok, w_vt, w_bat):
    b, s, d = x.shape
    tm = ROW_TILE
    n_tok = w_tok.shape[1]
    kern = functools.partial(_inproj_kernel, d_model=d)
    const = lambda bi, ti: (0, 0)
    row = lambda bi, ti: (bi, ti, 0)
    widths = [3 * DN_WIDTH, DN_WIDTH, 2 * MB_WIDTH, MB_WIDTH, 2 * d, LANES]
    dtypes = [BF16, BF16, BF16, BF16, BF16, F32]
    out_shape = [jax.ShapeDtypeStruct((b, s, w), t) for w, t in zip(widths, dtypes)]
    out_specs = [pl.BlockSpec((1, tm, w), row) for w in widths]
    out_shape += [
        jax.ShapeDtypeStruct((b, s // MB_BLOCK, MB_WIDTH, MB_BLOCK), BF16),
        jax.ShapeDtypeStruct((b, 2 * DN_HEADS, s), F32),
    ]
    out_specs += [
        pl.BlockSpec((1, tm // MB_BLOCK, MB_WIDTH, MB_BLOCK), lambda bi, ti: (bi, ti, 0, 0)),
        pl.BlockSpec((1, 2 * DN_HEADS, tm), lambda bi, ti: (bi, 0, ti)),
    ]
    return pl.pallas_call(
        kern,
        grid=(b, s // tm),
        in_specs=[
            pl.BlockSpec((1, tm, d), row),
            pl.BlockSpec((1, 1, 3 * d), lambda bi, ti: (bi, 0, 0)),
            pl.BlockSpec((1, d), const),
            pl.BlockSpec((d, n_tok), const, pipeline_mode=pl.Buffered(1)),
            pl.BlockSpec((MB_WIDTH, d), const, pipeline_mode=pl.Buffered(1)),
            pl.BlockSpec((2 * DN_HEADS, d), const, pipeline_mode=pl.Buffered(1)),
        ],
        out_specs=out_specs,
        out_shape=out_shape,
        compiler_params=pltpu.CompilerParams(
            dimension_semantics=("parallel", "parallel"),
            vmem_limit_bytes=V7X_VMEM_LIMIT_BYTES),
        name="in_projection",
    )(x, mod, g_pre, w_tok, w_vt, w_bat)


def _dn_kernel(qkv_ref, ba_ref, bat_ref, z_ref, convw_ref, pcol_ref, prow_ref, gnorm_ref,
               o_ref, xs_ref, state_ref):
    t = pl.program_id(1)
    tt = qkv_ref.shape[1]
    n_chunks = tt // DN_CHUNK
    dk = DN_HEAD_DIM

    @pl.when(t == 0)
    def _():
        xs_ref[0:CONV_HALO, :] = jnp.zeros((CONV_HALO, xs_ref.shape[1]), F32)
        state_ref[...] = jnp.zeros(state_ref.shape, F32)

    xs_ref[CONV_HALO:CONV_HALO + tt, :] = qkv_ref[0].astype(F32)

    def conv_silu(c0):
        acc = None
        for k in range(DN_CONV):
            r0 = CONV_HALO - (DN_CONV - 1) + k
            term = convw_ref[k:k + 1, c0:c0 + dk] * xs_ref[r0:r0 + tt, c0:c0 + dk]
            acc = term if acc is None else acc + term
        return _silu(acc)

    ba = ba_ref[0]
    beta_all = jax.nn.sigmoid(ba)
    g_col = -jnp.exp(pcol_ref[0:1, :]) * jax.nn.softplus(ba + pcol_ref[1:2, :])
    g_row = -jnp.exp(prow_ref[:, 0:1]) * jax.nn.softplus(bat_ref[0] + prow_ref[:, 1:2])

    ri = lax.broadcasted_iota(jnp.int32, (tt, tt), 0)
    ci = lax.broadcasted_iota(jnp.int32, (tt, tt), 1)
    same_chunk = (ri // DN_CHUNK) == (ci // DN_CHUNK)
    causal = same_chunk & (ri >= ci)
    strict = same_chunk & (ri > ci)
    lower = causal.astype(BF16)
    upper = (same_chunk & (ri <= ci)).astype(BF16)
    gch, gcl = _split_bf16(g_col)
    gc_col = _dot(lower, gch) + _dot(lower, gcl)
    grh, grl = _split_bf16(g_row)
    gc_row = _dot(grh, upper) + _dot(grl, upper)

    eg = jnp.exp(gc_col)
    g_last = [gc_col[(c + 1) * DN_CHUNK - 1:(c + 1) * DN_CHUNK, :] for c in range(n_chunks)]
    eg_to_last = jnp.concatenate(
        [jnp.exp(g_last[c] - gc_col[c * DN_CHUNK:(c + 1) * DN_CHUNK, :]) for c in range(n_chunks)], axis=0)
    eg_last = [jnp.exp(g_last[c]) for c in range(n_chunks)]

    eye = (ri == ci).astype(F32)
    col_chunk = lax.broadcasted_iota(jnp.int32, (dk, tt), 1) // DN_CHUNK

    heads = range(DN_HEADS)
    qn, kn, kb, vb, scores = [], [], [], [], []
    for h in heads:
        q = conv_silu(h * dk)
        k = conv_silu(DN_WIDTH + h * dk)
        v = conv_silu(2 * DN_WIDTH + h * dk)
        qn.append(q * lax.rsqrt(jnp.sum(q * q, axis=-1, keepdims=True) + NORM_EPS) * (dk ** -0.5))
        kn.append(k * lax.rsqrt(jnp.sum(k * k, axis=-1, keepdims=True) + NORM_EPS))
        beta = beta_all[:, h:h + 1]
        kb.append(kn[h] * beta)
        vb.append(v * beta)
        scores.append(_dot_nt(jnp.concatenate([kb[h].astype(BF16), qn[h].astype(BF16)], axis=0),
                              kn[h].astype(BF16)))

    a, qk = [], []
    for h in heads:
        gl = DN_HEADS + h
        diff = gc_col[:, gl:gl + 1] - gc_row[gl:gl + 1, :]
        decay = jnp.where(causal, jnp.exp(jnp.where(causal, diff, 0.0)), 0.0)
        a.append(jnp.where(strict, scores[h][0:tt] * decay, 0.0))
        qk.append((scores[h][tt:2 * tt] * decay).astype(BF16))

    def same_block(n):
        return (ri // n) == (ci // n)

    p = [eye - jnp.where(same_block(2), a[h], 0.0) for h in heads]
    n = 2
    while n < DN_CHUNK:
        lower_left = same_block(2 * n) & jnp.logical_not(same_block(n))
        p_b = [p[h].astype(BF16) for h in heads]
        mid = [_dot(jnp.where(lower_left, a[h], 0.0).astype(BF16), p_b[h]) for h in heads]
        upd = [_dot(p_b[h], mid[h].astype(BF16)) for h in heads]
        p = [p[h] - upd[h] for h in heads]
        n *= 2

    eg_h = [eg[:, DN_HEADS + h:DN_HEADS + h + 1] for h in heads]
    uw = [_dot(p[h].astype(BF16),
               jnp.concatenate([vb[h].astype(BF16), (kb[h] * eg_h[h]).astype(BF16)], axis=1)) for h in heads]
    u = [uw[h][:, 0:dk] for h in heads]
    w_b = [uw[h][:, dk:2 * dk].astype(BF16) for h in heads]
    qg_b = [(qn[h] * eg_h[h]).astype(BF16) for h in heads]
    kdt = [(kn[h] * eg_to_last[:, DN_HEADS + h:DN_HEADS + h + 1]).T for h in heads]

    state = [state_ref[h] for h in heads]
    v_rows = [[] for _ in heads]
    o_rows = [[] for _ in heads]
    for c in range(n_chunks):
        rows = slice(c * DN_CHUNK, (c + 1) * DN_CHUNK)
        r2 = [_dot(jnp.concatenate([w_b[h][rows], qg_b[h][rows]], axis=0), state[h].astype(BF16)) for h in heads]
        lhs3 = []
        for h in heads:
            kdt_c = jnp.where(col_chunk == c, kdt[h], 0.0).astype(BF16)
            lhs3.append(jnp.concatenate([qk[h][rows], kdt_c], axis=0))
        r3 = []
        for h in heads:
            v_rows[h].append((u[h][rows] - r2[h][0:DN_CHUNK]).astype(BF16))
            v_blk = jnp.concatenate(
                v_rows[h] + [jnp.zeros(((n_chunks - 1 - c) * DN_CHUNK, dk), BF16)] * (c < n_chunks - 1), axis=0)
            r3.append(_dot(lhs3[h], v_blk))
        for h in heads:
            o_rows[h].append(r2[h][DN_CHUNK:2 * DN_CHUNK] + r3[h][0:DN_CHUNK])
            state[h] = state[h] * eg_last[c][:, DN_HEADS + h:DN_HEADS + h + 1] + r3[h][DN_CHUNK:DN_CHUNK + dk]

    for h in heads:
        state_ref[h] = state[h]
        o = jnp.concatenate(o_rows[h], axis=0)
        on = o * lax.rsqrt(jnp.mean(o * o, axis=-1, keepdims=True) + NORM_EPS) * gnorm_ref[...]
        zh = z_ref[0, :, h * dk:(h + 1) * dk].astype(F32)
        o_ref[0, :, h * dk:(h + 1) * dk] = (on * _silu(zh)).astype(o_ref.dtype)

    xs_ref[0:CONV_HALO, :] = xs_ref[tt:tt + CONV_HALO, :]


def _deltanet(qkv, ba, bat, z, conv_w, p_col, p_row, g_norm):
    b, s, w3 = qkv.shape
    tt = DN_TILE
    row = lambda bi, ti: (bi, ti, 0)
    const = lambda bi, ti: (0, 0)
    return pl.pallas_call(
        _dn_kernel,
        grid=(b, s // tt),
        in_specs=[
            pl.BlockSpec((1, tt, w3), row),
            pl.BlockSpec((1, tt, LANES), row),
            pl.BlockSpec((1, 2 * DN_HEADS, tt), lambda bi, ti: (bi, 0, ti)),
            pl.BlockSpec((1, tt, DN_WIDTH), row),
            pl.BlockSpec((DN_CONV, w3), const),
            pl.BlockSpec((2, LANES), const),
            pl.BlockSpec((2 * DN_HEADS, 2), const),
            pl.BlockSpec((1, DN_HEAD_DIM), const),
        ],
        out_specs=pl.BlockSpec((1, tt, DN_WIDTH), row),
        out_shape=jax.ShapeDtypeStruct((b, s, DN_WIDTH), BF16),
        scratch_shapes=[
            pltpu.VMEM((CONV_HALO + tt, w3), F32),
            pltpu.VMEM((DN_HEADS, DN_HEAD_DIM, DN_HEAD_DIM), F32),
        ],
        compiler_params=pltpu.CompilerParams(
            dimension_semantics=("parallel", "arbitrary"),
            vmem_limit_bytes=V7X_VMEM_LIMIT_BYTES),
        name="deltanet",
    )(qkv, ba, bat, z, conv_w, p_col, p_row, g_norm)


def _moba_kernel(q_ref, k_ref, vt_ref, z_ref, o_ref, kmean_ref, kaug_ref):
    i = pl.program_id(2)
    nb = vt_ref.shape[1]
    blk = MB_BLOCK
    dh = MB_HEAD_DIM
    heads = LANES // dh
    lane = lax.broadcasted_iota(jnp.int32, (1, LANES), 1)
    in_head = [(lane >= h * dh) & (lane < (h + 1) * dh) for h in range(heads)]
    bias_lane0 = [((h + 1) % heads) * dh for h in range(heads)]

    @pl.when(i == 0)
    def _():
        for j in range(nb):
            kj = k_ref[0, j * blk:(j + 1) * blk, :]
            kmean_ref[j:j + 1, :] = jnp.mean(kj.astype(F32), axis=0, keepdims=True)
            for h in range(heads):
                onehot = (lane == bias_lane0[h] + j).astype(BF16)
                kaug_ref[h, j * blk:(j + 1) * blk, :] = jnp.where(in_head[h], kj, onehot)
        for h in range(heads):
            kaug_ref[h, nb * blk:(nb + 1) * blk, :] = jnp.broadcast_to(
                (lane == bias_lane0[h] + nb).astype(BF16), (blk, LANES))

    q = q_ref[0]
    jidx = lax.broadcasted_iota(jnp.int32, (nb, blk), 0)
    valid = jidx < i
    km = kmean_ref[...]
    qf = q.astype(F32)

    bias = []
    for h in range(heads):
        gate = lax.dot_general(jnp.where(in_head[h], km, 0.0), qf, NT_DIMS,
                               preferred_element_type=F32, precision=lax.Precision.HIGHEST)
        gate = jnp.where(valid, gate, -jnp.inf)
        rank = jnp.zeros((nb, blk), jnp.int32)
        for jp in range(nb):
            other = gate[jp:jp + 1, :]
            beats = (other > gate) | ((other == gate) & (jp < jidx))
            rank = rank + beats.astype(jnp.int32)
        keep = ((rank < MB_TOPK) & valid) | (jidx == i)
        bias.append(jnp.where(keep, 0.0, MASK_VALUE))
    pad_rows = dh - nb
    bias_rows = []
    for h in range(heads):
        other = (h + 1) % heads
        bias_rows += [bias[other], jnp.full((1, blk), MASK_VALUE, F32), jnp.zeros((pad_rows - 1, blk), F32)]
    bias_t = jnp.concatenate(bias_rows, axis=0).T.astype(BF16)
    q_aug = [jnp.where(in_head[h], q, bias_t) for h in range(heads)]
    ones_rows = jnp.ones((BF16_SUBLANES, blk), BF16)

    def v_ext(jb, h):
        return jnp.concatenate([vt_ref[0, jb, h * dh:(h + 1) * dh, :], ones_rows], axis=0)

    def scores(t):
        tiles, maxima = [], []
        for h in range(heads):
            cm = None
            for g in range(KV_GROUP):
                j = t * KV_GROUP + g
                kb = jnp.where(j < i, j, nb)
                st = _dot_nt(kaug_ref[h, pl.ds(pl.multiple_of(kb * blk, blk), blk), :], q_aug[h])
                tiles.append(st)
                gm = jnp.max(st, axis=0, keepdims=True)
                cm = gm if cm is None else jnp.maximum(cm, gm)
            maxima.append(cm)
        return tiles, maxima

    k_own = [kaug_ref[h, pl.ds(pl.multiple_of(i * blk, blk), blk), :] for h in range(heads)]
    key_pos = lax.broadcasted_iota(jnp.int32, (blk, blk), 0)
    qry_pos = lax.broadcasted_iota(jnp.int32, (blk, blk), 1)
    sts = [_dot_nt(k_own[h], q_aug[h]) for h in range(heads)]
    tiles, cms = scores(0)
    ms, ps = [], []
    for h in range(heads):
        st = jnp.where(key_pos <= qry_pos, sts[h], MASK_VALUE)
        m = jnp.max(st, axis=0, keepdims=True)
        ms.append(m)
        ps.append(jnp.exp2(st - m).astype(BF16))
    carry = []
    for h in range(heads):
        carry += [ms[h], _dot(v_ext(i, h), ps[h]), cms[h]]
    carry += tiles

    def body(t, carry):
        tiles_next, cm_next = scores(t + 1)
        tiles = carry[3 * heads:]
        new_m, alphas, ps = [], [], []
        for h in range(heads):
            m, _, cm = carry[3 * h:3 * h + 3]
            m_new = jnp.maximum(m, cm)
            new_m.append(m_new)
            alphas.append(jnp.exp2(m - m_new))
            ps.append([jnp.exp2(tiles[h * KV_GROUP + g] - m_new).astype(BF16) for g in range(KV_GROUP)])
        out = []
        for h in range(heads):
            pv = None
            for g in range(KV_GROUP):
                vb = jnp.minimum(t * KV_GROUP + g, nb - 1)
                term = _dot(v_ext(vb, h), ps[h][g])
                pv = term if pv is None else pv + term
            out += [new_m[h], alphas[h] * carry[3 * h + 1] + pv, cm_next[h]]
        return tuple(out) + tuple(tiles_next)

    carry = lax.fori_loop(0, (i + KV_GROUP - 1) // KV_GROUP, body, tuple(carry))
    ot = jnp.concatenate([carry[3 * h + 1][0:dh] / carry[3 * h + 1][dh:dh + 1] for h in range(heads)], axis=0)
    o = ot.T
    o_ref[0] = (o * _silu(z_ref[0].astype(F32))).astype(o_ref.dtype)


def _moba(qk, vt, z):
    b, s, _ = qk.shape
    nb = s // MB_BLOCK
    pairs = MB_WIDTH // LANES
    heads = LANES // MB_HEAD_DIM
    assert nb + 1 <= MB_HEAD_DIM
    return pl.pallas_call(
        _moba_kernel,
        grid=(b, pairs, nb),
        in_specs=[
            pl.BlockSpec((1, MB_BLOCK, LANES), lambda bi, hp, i: (bi, i, hp)),
            pl.BlockSpec((1, s, LANES), lambda bi, hp, i: (bi, 0, pairs + hp)),
            pl.BlockSpec((1, nb, LANES, MB_BLOCK), lambda bi, hp, i: (bi, 0, hp, 0)),
            pl.BlockSpec((1, MB_BLOCK, LANES), lambda bi, hp, i: (bi, i, hp)),
        ],
        out_specs=pl.BlockSpec((1, MB_BLOCK, LANES), lambda bi, hp, i: (bi, i, hp)),
        out_shape=jax.ShapeDtypeStruct((b, s, MB_WIDTH), BF16),
        scratch_shapes=[
            pltpu.VMEM((nb, LANES), F32),
            pltpu.VMEM((heads, s + MB_BLOCK, LANES), BF16),
        ],
        compiler_params=pltpu.CompilerParams(
            dimension_semantics=("parallel", "parallel", "arbitrary"),
            vmem_limit_bytes=V7X_VMEM_LIMIT_BYTES),
        name="moba_attention",
    )(qk, qk, vt, z)


def _outproj_kernel(x_ref, odn_ref, omb_ref, merge_ref, mod_ref, gpost_ref,
                    wdn_ref, wmb_ref, wout_ref, o_ref, *, d_model):
    y_dn = _dot(odn_ref[0], wdn_ref[...])
    y_mb = _dot(omb_ref[0], wmb_ref[...])
    gates = jax.nn.sigmoid(merge_ref[0].astype(F32))
    mixed_in = gates[:, 0:d_model] * y_dn + gates[:, d_model:2 * d_model] * y_mb
    mixed = _dot(mixed_in.astype(BF16), wout_ref[...])
    y = mixed * lax.rsqrt(jnp.mean(mixed * mixed, axis=-1, keepdims=True) + NORM_EPS) * gpost_ref[...]
    gate = mod_ref[0, :, 2 * d_model:3 * d_model]
    o_ref[0] = x_ref[0] + gate * y


def _out_projection(x, o_dn, o_mb, merge, mod, g_post, w_dn, w_mb, w_out):
    b, s, d = x.shape
    tm = ROW_TILE
    row = lambda bi, ti: (bi, ti, 0)
    const = lambda bi, ti: (0, 0)
    kern = functools.partial(_outproj_kernel, d_model=d)
    return pl.pallas_call(
        kern,
        grid=(b, s // tm),
        in_specs=[
            pl.BlockSpec((1, tm, d), row),
            pl.BlockSpec((1, tm, DN_WIDTH), row),
            pl.BlockSpec((1, tm, MB_WIDTH), row),
            pl.BlockSpec((1, tm, 2 * d), row),
            pl.BlockSpec((1, 1, 3 * d), lambda bi, ti: (bi, 0, 0)),
            pl.BlockSpec((1, d), const),
            pl.BlockSpec((DN_WIDTH, d), const),
            pl.BlockSpec((MB_WIDTH, d), const),
            pl.BlockSpec((d, d), const),
        ],
        out_specs=pl.BlockSpec((1, tm, d), row),
        out_shape=jax.ShapeDtypeStruct((b, s, d), F32),
        compiler_params=pltpu.CompilerParams(
            dimension_semantics=("parallel", "parallel"),
            vmem_limit_bytes=V7X_VMEM_LIMIT_BYTES),
        name="out_projection",
    )(x, o_dn, o_mb, merge, mod, g_post, w_dn, w_mb, w_out)


def _split_w_in(w_in, d_model):
    sizes = (3 * DN_WIDTH, DN_WIDTH, DN_HEADS, DN_HEADS, 3 * MB_WIDTH, MB_WIDTH, 2 * d_model)
    cuts = [0]
    for n in sizes:
        cuts.append(cuts[-1] + n)
    w_qkv_dn, w_z_dn, w_beta, w_a, w_qkv_mb, w_z_mb, w_merge = (
        w_in[:, cuts[i]:cuts[i + 1]] for i in range(len(sizes)))
    w_ba = jnp.concatenate([w_beta, w_a], axis=1)
    w_ba_pad = jnp.pad(w_ba, ((0, 0), (0, LANES - 2 * DN_HEADS)))
    w_q_mb = w_qkv_mb[:, 0:MB_WIDTH] * (MB_HEAD_DIM ** -0.5 * LOG2_E)
    w_k_mb = w_qkv_mb[:, MB_WIDTH:2 * MB_WIDTH]
    w_tok = jnp.concatenate(
        [w_qkv_dn, w_z_dn, w_q_mb, w_k_mb, w_z_mb, w_merge, w_ba_pad], axis=1).astype(BF16)
    w_vt = w_qkv_mb[:, 2 * MB_WIDTH:3 * MB_WIDTH].T.astype(BF16)
    w_bat = w_ba.T.astype(BF16)
    return w_tok, w_vt, w_bat


def _layer(x, mod, g_pre, g_post, w_in, conv_w, a_log, dt_bias, dn_norm_g, w_proj_dn, w_proj_mb, w_out):
    b, s, d = x.shape
    w_tok, w_vt, w_bat = _split_w_in(w_in, d)
    qkv_dn, z_dn, qk_mb, z_mb, merge, ba, vt, bat = _in_projection(
        x, mod, g_pre.reshape(1, d), w_tok, w_vt, w_bat)

    zeros_h = jnp.zeros((DN_HEADS,), F32)
    a_vec = jnp.concatenate([zeros_h, a_log.astype(F32)])
    dt_vec = jnp.concatenate([zeros_h, dt_bias.astype(F32)])
    p_col = jnp.pad(jnp.stack([a_vec, dt_vec]), ((0, 0), (0, LANES - 2 * DN_HEADS)))
    p_row = jnp.stack([a_vec, dt_vec], axis=1)
    o_dn = _deltanet(qkv_dn, ba, bat, z_dn, conv_w.astype(F32), p_col, p_row,
                     dn_norm_g.reshape(1, DN_HEAD_DIM).astype(F32))
    o_mb = _moba(qk_mb, vt, z_mb)
    return _out_projection(x, o_dn, o_mb, merge, mod, g_post.reshape(1, d),
                           w_proj_dn.astype(BF16), w_proj_mb.astype(BF16), w_out.astype(BF16))


def kernel(x, c, w_ada, b_ada, g_pre, g_post, w_in, conv_w, a_log, dt_bias, dn_norm_g, w_proj_dn, w_proj_mb, w_out):
    depth = w_ada.shape[0]
    b, s, d = x.shape
    assert s % ROW_TILE == 0 and s % MB_BLOCK == 0 and s % DN_TILE == 0
    mod_all = _modulation(c, w_ada, b_ada)
    for l in range(depth):
        mod = mod_all[l].reshape(b, 1, 3 * d)
        x = _layer(x, mod, g_pre[l], g_post[l], w_in[l], conv_w[l], a_log[l], dt_bias[l],
                   dn_norm_g[l], w_proj_dn[l], w_proj_mb[l], w_out[l])
    return x
```

```python
import functools

import jax
import jax.numpy as jnp
from jax import lax
from jax.experimental import pallas as pl
from jax.experimental.pallas import tpu as pltpu

F32 = jnp.float32
BF16 = jnp.bfloat16

NORM_EPS = 1e-6
DN_HEADS = 4
DN_HEAD_DIM = 128
DN_WIDTH = DN_HEADS * DN_HEAD_DIM
DN_CONV = 4
DN_CHUNK = 64
MB_HEADS = 8
MB_HEAD_DIM = 64
MB_WIDTH = MB_HEADS * MB_HEAD_DIM
MB_BLOCK = 256
MB_TOPK = 3

V7X_VMEM_LIMIT_BYTES = 56 * 1024 * 1024
LANES = 128
CONV_HALO = 8

ROW_TILE = 512
DN_TILE = 256
KV_GROUP = 3
BF16_SUBLANES = 16
MASK_VALUE = -1e30
LOG2_E = 1.4426950408889634

NT_DIMS = (((1,), (1,)), ((), ()))


def _silu(v):
    return v * jax.nn.sigmoid(v)


def _dot(a, b):
    return jnp.dot(a, b, preferred_element_type=F32)


def _dot_nt(a, b):
    return lax.dot_general(a, b, NT_DIMS, preferred_element_type=F32)


def _split_bf16(v):
    hi = v.astype(BF16)
    lo = (v - hi.astype(F32)).astype(BF16)
    return hi, lo


def _mod_kernel(c_ref, w_ref, b_ref, o_ref):
    sc = _silu(c_ref[...])
    o_ref[0] = jnp.dot(sc, w_ref[0], preferred_element_type=F32,
                       precision=lax.Precision.HIGHEST) + b_ref[0]


def _modulation(c, w_ada, b_ada):
    depth, d, d3 = w_ada.shape
    b = c.shape[0]
    tn = 512
    return pl.pallas_call(
        _mod_kernel,
        grid=(depth, d3 // tn),
        in_specs=[
            pl.BlockSpec((b, d), lambda l, j: (0, 0)),
            pl.BlockSpec((1, d, tn), lambda l, j: (l, 0, j)),
            pl.BlockSpec((1, 1, tn), lambda l, j: (l, 0, j)),
        ],
        out_specs=pl.BlockSpec((1, b, tn), lambda l, j: (l, 0, j)),
        out_shape=jax.ShapeDtypeStruct((depth, b, d3), F32),
        name="adaln_modulation",
    )(c, w_ada, b_ada.reshape(depth, 1, d3))


def _inproj_kernel(x_ref, mod_ref, gpre_ref, wtok_ref, wvt_ref, wbat_ref,
                   qkv_dn_ref, z_dn_ref, qk_mb_ref, z_mb_ref, merge_ref, ba_ref, vt_ref, bat_ref,
                   *, d_model):
    x = x_ref[0]
    shift = mod_ref[0, :, 0:d_model]
    scale = mod_ref[0, :, d_model:2 * d_model]
    xn = x * lax.rsqrt(jnp.mean(x * x, axis=-1, keepdims=True) + NORM_EPS)
    h = (xn * gpre_ref[...]) * (1.0 + scale) + shift
    hb = h.astype(BF16)

    col = 0
    for ref in (qkv_dn_ref, z_dn_ref, qk_mb_ref, z_mb_ref, merge_ref, ba_ref):
        width = ref.shape[-1]
        step = min(width, 512)
        for c0 in range(0, width, step):
            ref[0, :, c0:c0 + step] = _dot(hb, wtok_ref[:, col + c0:col + c0 + step]).astype(ref.dtype)
        col += width

    vt = _dot_nt(wvt_ref[...], hb).astype(BF16)
    for j in range(vt_ref.shape[1]):
        vt_ref[0, j] = vt[:, j * MB_BLOCK:(j + 1) * MB_BLOCK]
    bat_ref[0] = _dot_nt(wbat_ref[...], hb)


def _in_projection(x, mod, g_pre, w_tok, w_vt, w_bat):
    b, s, d = x.shape
    tm = ROW_TILE
    n_tok = w_tok.shape[1]
    kern = functools.partial(_inproj_kernel, d_model=d)
    const = lambda bi, ti: (0, 0)
    row = lambda bi, ti: (bi, ti, 0)
    widths = [3 * DN_WIDTH, DN_WIDTH, 2 * MB_WIDTH, MB_WIDTH, 2 * d, LANES]
    dtypes = [BF16, BF16, BF16, BF16, BF16, F32]
    out_shape = [jax.ShapeDtypeStruct((b, s, w), t) for w, t in zip(widths, dtypes)]
    out_specs = [pl.BlockSpec((1, tm, w), row) for w in widths]
    out_shape += [
        jax.ShapeDtypeStruct((b, s // MB_BLOCK, MB_WIDTH, MB_BLOCK), BF16),
        jax.ShapeDtypeStruct((b, 2 * DN_HEADS, s), F32),
    ]
    out_specs += [
        pl.BlockSpec((1, tm // MB_BLOCK, MB_WIDTH, MB_BLOCK), lambda bi, ti: (bi, ti, 0, 0)),
        pl.BlockSpec((1, 2 * DN_HEADS, tm), lambda bi, ti: (bi, 0, ti)),
    ]
    return pl.pallas_call(
        kern,
        grid=(b, s // tm),
        in_specs=[
            pl.BlockSpec((1, tm, d), row),
            pl.BlockSpec((1, 1, 3 * d), lambda bi, ti: (bi, 0, 0)),
            pl.BlockSpec((1, d), const),
            pl.BlockSpec((d, n_tok), const, pipeline_mode=pl.Buffered(1)),
            pl.BlockSpec((MB_WIDTH, d), const, pipeline_mode=pl.Buffered(1)),
            pl.BlockSpec((2 * DN_HEADS, d), const, pipeline_mode=pl.Buffered(1)),
        ],
        out_specs=out_specs,
        out_shape=out_shape,
        compiler_params=pltpu.CompilerParams(
            dimension_semantics=("parallel", "parallel"),
            vmem_limit_bytes=V7X_VMEM_LIMIT_BYTES),
        name="in_projection",
    )(x, mod, g_pre, w_tok, w_vt, w_bat)


def _dn_kernel(qkv_ref, ba_ref, bat_ref, z_ref, convw_ref, pcol_ref, prow_ref, gnorm_ref,
               o_ref, xs_ref, state_ref):
    t = pl.program_id(1)
    tt = qkv_ref.shape[1]
    n_chunks = tt // DN_CHUNK
    dk = DN_HEAD_DIM

    @pl.when(t == 0)
    def _():
        xs_ref[0:CONV_HALO, :] = jnp.zeros((CONV_HALO, xs_ref.shape[1]), F32)
        state_ref[...] = jnp.zeros(state_ref.shape, F32)

    xs_ref[CONV_HALO:CONV_HALO + tt, :] = qkv_ref[0].astype(F32)

    def conv_silu(c0):
        acc = None
        for k in range(DN_CONV):
            r0 = CONV_HALO - (DN_CONV - 1) + k
            term = convw_ref[k:k + 1, c0:c0 + dk] * xs_ref[r0:r0 + tt, c0:c0 + dk]
            acc = term if acc is None else acc + term
        return _silu(acc)

    ba = ba_ref[0]
    beta_all = jax.nn.sigmoid(ba)
    g_col = -jnp.exp(pcol_ref[0:1, :]) * jax.nn.softplus(ba + pcol_ref[1:2, :])
    g_row = -jnp.exp(prow_ref[:, 0:1]) * jax.nn.softplus(bat_ref[0] + prow_ref[:, 1:2])

    ri = lax.broadcasted_iota(jnp.int32, (tt, tt), 0)
    ci = lax.broadcasted_iota(jnp.int32, (tt, tt), 1)
    same_chunk = (ri // DN_CHUNK) == (ci // DN_CHUNK)
    causal = same_chunk & (ri >= ci)
    strict = same_chunk & (ri > ci)
    lower = causal.astype(BF16)
    upper = (same_chunk & (ri <= ci)).astype(BF16)
    gch, gcl = _split_bf16(g_col)
    gc_col = _dot(lower, gch) + _dot(lower, gcl)
    grh, grl = _split_bf16(g_row)
    gc_row = _dot(grh, upper) + _dot(grl, upper)

    eg = jnp.exp(gc_col)
    g_last = [gc_col[(c + 1) * DN_CHUNK - 1:(c + 1) * DN_CHUNK, :] for c in range(n_chunks)]
    eg_to_last = jnp.concatenate(
        [jnp.exp(g_last[c] - gc_col[c * DN_CHUNK:(c + 1) * DN_CHUNK, :]) for c in range(n_chunks)], axis=0)
    eg_last = [jnp.exp(g_last[c]) for c in range(n_chunks)]

    eye = (ri == ci).astype(F32)
    col_chunk = lax.broadcasted_iota(jnp.int32, (dk, tt), 1) // DN_CHUNK

    heads = range(DN_HEADS)
    qn, kn, kb, vb, scores = [], [], [], [], []
    for h in heads:
        q = conv_silu(h * dk)
        k = conv_silu(DN_WIDTH + h * dk)
        v = conv_silu(2 * DN_WIDTH + h * dk)
        qn.append(q * lax.rsqrt(jnp.sum(q * q, axis=-1, keepdims=True) + NORM_EPS) * (dk ** -0.5))
        kn.append(k * lax.rsqrt(jnp.sum(k * k, axis=-1, keepdims=True) + NORM_EPS))
        beta = beta_all[:, h:h + 1]
        kb.append(kn[h] * beta)
        vb.append(v * beta)
        scores.append(_dot_nt(jnp.concatenate([kb[h].astype(BF16), qn[h].astype(BF16)], axis=0),
                              kn[h].astype(BF16)))

    a, qk = [], []
    for h in heads:
        gl = DN_HEADS + h
        diff = gc_col[:, gl:gl + 1] - gc_row[gl:gl + 1, :]
        decay = jnp.where(causal, jnp.exp(jnp.where(causal, diff, 0.0)), 0.0)
        a.append(jnp.where(strict, scores[h][0:tt] * decay, 0.0))
        qk.append((scores[h][tt:2 * tt] * decay).astype(BF16))

    def same_block(n):
        return (ri // n) == (ci // n)

    p = [eye - jnp.where(same_block(2), a[h], 0.0) for h in heads]
    n = 2
    while n < DN_CHUNK:
        lower_left = same_block(2 * n) & jnp.logical_not(same_block(n))
        p_b = [p[h].astype(BF16) for h in heads]
        mid = [_dot(jnp.where(lower_left, a[h], 0.0).astype(BF16), p_b[h]) for h in heads]
        upd = [_dot(p_b[h], mid[h].astype(BF16)) for h in heads]
        p = [p[h] - upd[h] for h in heads]
        n *= 2

    eg_h = [eg[:, DN_HEADS + h:DN_HEADS + h + 1] for h in heads]
    uw = [_dot(p[h].astype(BF16),
               jnp.concatenate([vb[h].astype(BF16), (kb[h] * eg_h[h]).astype(BF16)], axis=1)) for h in heads]
    u = [uw[h][:, 0:dk] for h in heads]
    w_b = [uw[h][:, dk:2 * dk].astype(BF16) for h in heads]
    qg_b = [(qn[h] * eg_h[h]).astype(BF16) for h in heads]
    kdt = [(kn[h] * eg_to_last[:, DN_HEADS + h:DN_HEADS + h + 1]).T for h in heads]

    state = [state_ref[h] for h in heads]
    v_rows = [[] for _ in heads]
    o_rows = [[] for _ in heads]
    for c in range(n_chunks):
        rows = slice(c * DN_CHUNK, (c + 1) * DN_CHUNK)
        r2 = [_dot(jnp.concatenate([w_b[h][rows], qg_b[h][rows]], axis=0), state[h].astype(BF16)) for h in heads]
        lhs3 = []
        for h in heads:
            kdt_c = jnp.where(col_chunk == c, kdt[h], 0.0).astype(BF16)
            lhs3.append(jnp.concatenate([qk[h][rows], kdt_c], axis=0))
        r3 = []
        for h in heads:
            v_rows[h].append((u[h][rows] - r2[h][0:DN_CHUNK]).astype(BF16))
            v_blk = jnp.concatenate(
                v_rows[h] + [jnp.zeros(((n_chunks - 1 - c) * DN_CHUNK, dk), BF16)] * (c < n_chunks - 1), axis=0)
            r3.append(_dot(lhs3[h], v_blk))
        for h in heads:
            o_rows[h].append(r2[h][DN_CHUNK:2 * DN_CHUNK] + r3[h][0:DN_CHUNK])
            state[h] = state[h] * eg_last[c][:, DN_HEADS + h:DN_HEADS + h + 1] + r3[h][DN_CHUNK:DN_CHUNK + dk]

    for h in heads:
        state_ref[h] = state[h]
        o = jnp.concatenate(o_rows[h], axis=0)
        on = o * lax.rsqrt(jnp.mean(o * o, axis=-1, keepdims=True) + NORM_EPS) * gnorm_ref[...]
        zh = z_ref[0, :, h * dk:(h + 1) * dk].astype(F32)
        o_ref[0, :, h * dk:(h + 1) * dk] = (on * _silu(zh)).astype(o_ref.dtype)

    xs_ref[0:CONV_HALO, :] = xs_ref[tt:tt + CONV_HALO, :]


def _deltanet(qkv, ba, bat, z, conv_w, p_col, p_row, g_norm):
    b, s, w3 = qkv.shape
    tt = DN_TILE
    row = lambda bi, ti: (bi, ti, 0)
    const = lambda bi, ti: (0, 0)
    return pl.pallas_call(
        _dn_kernel,
        grid=(b, s // tt),
        in_specs=[
            pl.BlockSpec((1, tt, w3), row),
            pl.BlockSpec((1, tt, LANES), row),
            pl.BlockSpec((1, 2 * DN_HEADS, tt), lambda bi, ti: (bi, 0, ti)),
            pl.BlockSpec((1, tt, DN_WIDTH), row),
            pl.BlockSpec((DN_CONV, w3), const),
            pl.BlockSpec((2, LANES), const),
            pl.BlockSpec((2 * DN_HEADS, 2), const),
            pl.BlockSpec((1, DN_HEAD_DIM), const),
        ],
        out_specs=pl.BlockSpec((1, tt, DN_WIDTH), row),
        out_shape=jax.ShapeDtypeStruct((b, s, DN_WIDTH), BF16),
        scratch_shapes=[
            pltpu.VMEM((CONV_HALO + tt, w3), F32),
            pltpu.VMEM((DN_HEADS, DN_HEAD_DIM, DN_HEAD_DIM), F32),
        ],
        compiler_params=pltpu.CompilerParams(
            dimension_semantics=("parallel", "arbitrary"),
            vmem_limit_bytes=V7X_VMEM_LIMIT_BYTES),
        name="deltanet",
    )(qkv, ba, bat, z, conv_w, p_col, p_row, g_norm)


def _moba_kernel(q_ref, k_ref, vt_ref, z_ref, o_ref, kmean_ref, kaug_ref, qaug_ref, s_ref):
    i = pl.program_id(2)
    nb = vt_ref.shape[1]
    blk = MB_BLOCK
    dh = MB_HEAD_DIM
    heads = LANES // dh
    lane = lax.broadcasted_iota(jnp.int32, (1, LANES), 1)
    in_head = [(lane >= h * dh) & (lane < (h + 1) * dh) for h in range(heads)]
    bias_lane0 = [((h + 1) % heads) * dh for h in range(heads)]

    @pl.when(i == 0)
    def _():
        for j in range(nb):
            kj = k_ref[0, j * blk:(j + 1) * blk, :]
            kmean_ref[j:j + 1, :] = jnp.mean(kj.astype(F32), axis=0, keepdims=True)
            for h in range(heads):
                onehot = (lane == bias_lane0[h] + j).astype(BF16)
                kaug_ref[h, j * blk:(j + 1) * blk, :] = jnp.where(in_head[h], kj, onehot)
        for h in range(heads):
            kaug_ref[h, nb * blk:(nb + 1) * blk, :] = jnp.broadcast_to(
                (lane == bias_lane0[h] + nb).astype(BF16), (blk, LANES))
        km = kmean_ref[...]
        pieces = []
        for h in range(heads):
            rest = jnp.where(in_head[h], km, 0.0)
            for _ in range(3):
                part = rest.astype(BF16)
                pieces.append(part)
                rest = rest - part.astype(F32)
        km_parts = jnp.concatenate(pieces, axis=0)
        jidx = lax.broadcasted_iota(jnp.int32, (nb, blk), 0)
        gates = []
        for iq in range(nb):
            g_all = _dot_nt(km_parts, q_ref[0, iq * blk:(iq + 1) * blk, :])
            gates.append([g_all[(3 * h) * nb:(3 * h + 1) * nb] + g_all[(3 * h + 1) * nb:(3 * h + 2) * nb]
                          + g_all[(3 * h + 2) * nb:(3 * h + 3) * nb] for h in range(heads)])
        pad_rows = dh - nb
        for iq in range(nb):
            bias = []
            for h in range(heads):
                gate = jnp.where(jidx < iq, gates[iq][h], -jnp.inf)
                rank = jnp.zeros((nb, blk), jnp.int32)
                for jp in range(iq):
                    other = gate[jp:jp + 1, :]
                    beats = (other > gate) | ((other == gate) & (jp < jidx))
                    rank = rank + beats.astype(jnp.int32)
                keep = ((rank < MB_TOPK) & (jidx < iq)) | (jidx == iq)
                bias.append(jnp.where(keep, 0.0, MASK_VALUE))
            bias_rows = []
            for h in range(heads):
                other = (h + 1) % heads
                bias_rows += [bias[other], jnp.full((1, blk), MASK_VALUE, F32), jnp.zeros((pad_rows - 1, blk), F32)]
            bias_t = jnp.concatenate(bias_rows, axis=0).T.astype(BF16)
            q_blk = q_ref[0, iq * blk:(iq + 1) * blk, :]
            for h in range(heads):
                qaug_ref[h, iq * blk:(iq + 1) * blk, :] = jnp.where(in_head[h], q_blk, bias_t)

    q_aug = [qaug_ref[h, pl.ds(pl.multiple_of(i * blk, blk), blk), :] for h in range(heads)]
    ones_rows = jnp.ones((BF16_SUBLANES, blk), BF16)

    def v_ext(jb, h):
        return jnp.concatenate([vt_ref[0, jb, h * dh:(h + 1) * dh, :], ones_rows], axis=0)

    def score_dots(t):
        tiles = []
        for h in range(heads):
            for g in range(KV_GROUP):
                j = t * KV_GROUP + g
                kb = jnp.where(j < i, j, nb)
                tiles.append(_dot_nt(kaug_ref[h, pl.ds(pl.multiple_of(kb * blk, blk), blk), :], q_aug[h]))
        return tiles

    def stash(tiles):
        maxima = []
        for h in range(heads):
            cm = None
            for g in range(KV_GROUP):
                st = tiles[h * KV_GROUP + g]
                s_ref[h * KV_GROUP + g] = st
                gm = jnp.max(st, axis=0, keepdims=True)
                cm = gm if cm is None else jnp.maximum(cm, gm)
            maxima.append(cm)
        return maxima

    k_own = [kaug_ref[h, pl.ds(pl.multiple_of(i * blk, blk), blk), :] for h in range(heads)]
    key_pos = lax.broadcasted_iota(jnp.int32, (blk, blk), 0)
    qry_pos = lax.broadcasted_iota(jnp.int32, (blk, blk), 1)
    sts = [_dot_nt(k_own[h], q_aug[h]) for h in range(heads)]
    tiles0 = score_dots(0)
    ms, ps = [], []
    for h in range(heads):
        st = jnp.where(key_pos <= qry_pos, sts[h], MASK_VALUE)
        m = jnp.max(st, axis=0, keepdims=True)
        ms.append(m)
        ps.append(jnp.exp2(st - m).astype(BF16))
    cms = stash(tiles0)
    carry = []
    for h in range(heads):
        carry += [ms[h], _dot(v_ext(i, h), ps[h]), cms[h]]

    def body(t, carry):
        tiles_next = score_dots(t + 1)
        new_m, alphas, ps = [], [], []
        for h in range(heads):
            m, _, cm = carry[3 * h:3 * h + 3]
            m_new = jnp.maximum(m, cm)
            new_m.append(m_new)
            alphas.append(jnp.exp2(m - m_new))
            ps.append([jnp.exp2(s_ref[h * KV_GROUP + g] - m_new).astype(BF16) for g in range(KV_GROUP)])
        cm_next = stash(tiles_next)
        out = []
        for h in range(heads):
            pv = None
            for g in range(KV_GROUP):
                vb = jnp.minimum(t * KV_GROUP + g, nb - 1)
                term = _dot(v_ext(vb, h), ps[h][g])
                pv = term if pv is None else pv + term
            out += [new_m[h], alphas[h] * carry[3 * h + 1] + pv, cm_next[h]]
        return tuple(out)

    carry = lax.fori_loop(0, (i + KV_GROUP - 1) // KV_GROUP, body, tuple(carry))
    ot = jnp.concatenate([carry[3 * h + 1][0:dh] / carry[3 * h + 1][dh:dh + 1] for h in range(heads)], axis=0)
    o = ot.T
    o_ref[0] = (o * _silu(z_ref[0].astype(F32))).astype(o_ref.dtype)


def _moba(qk, vt, z):
    b, s, _ = qk.shape
    nb = s // MB_BLOCK
    pairs = MB_WIDTH // LANES
    heads = LANES // MB_HEAD_DIM
    assert nb + 1 <= MB_HEAD_DIM
    return pl.pallas_call(
        _moba_kernel,
        grid=(b, pairs, nb),
        in_specs=[
            pl.BlockSpec((1, s, LANES), lambda bi, hp, i: (bi, 0, hp)),
            pl.BlockSpec((1, s, LANES), lambda bi, hp, i: (bi, 0, pairs + hp)),
            pl.BlockSpec((1, nb, LANES, MB_BLOCK), lambda bi, hp, i: (bi, 0, hp, 0)),
            pl.BlockSpec((1, MB_BLOCK, LANES), lambda bi, hp, i: (bi, i, hp)),
        ],
        out_specs=pl.BlockSpec((1, MB_BLOCK, LANES), lambda bi, hp, i: (bi, i, hp)),
        out_shape=jax.ShapeDtypeStruct((b, s, MB_WIDTH), BF16),
        scratch_shapes=[
            pltpu.VMEM((nb, LANES), F32),
            pltpu.VMEM((heads, s + MB_BLOCK, LANES), BF16),
            pltpu.VMEM((heads, s, LANES), BF16),
            pltpu.VMEM((heads * KV_GROUP, MB_BLOCK, MB_BLOCK), F32),
        ],
        compiler_params=pltpu.CompilerParams(
            dimension_semantics=("parallel", "parallel", "arbitrary"),
            vmem_limit_bytes=V7X_VMEM_LIMIT_BYTES),
        name="moba_attention",
    )(qk, qk, vt, z)


def _outproj_kernel(x_ref, odn_ref, omb_ref, merge_ref, mod_ref, gpost_ref,
                    wdn_ref, wmb_ref, wout_ref, o_ref, *, d_model):
    y_dn = _dot(odn_ref[0], wdn_ref[...])
    y_mb = _dot(omb_ref[0], wmb_ref[...])
    gates = jax.nn.sigmoid(merge_ref[0].astype(F32))
    mixed_in = gates[:, 0:d_model] * y_dn + gates[:, d_model:2 * d_model] * y_mb
    mixed = _dot(mixed_in.astype(BF16), wout_ref[...])
    y = mixed * lax.rsqrt(jnp.mean(mixed * mixed, axis=-1, keepdims=True) + NORM_EPS) * gpost_ref[...]
    gate = mod_ref[0, :, 2 * d_model:3 * d_model]
    o_ref[0] = x_ref[0] + gate * y


def _out_projection(x, o_dn, o_mb, merge, mod, g_post, w_dn, w_mb, w_out):
    b, s, d = x.shape
    tm = ROW_TILE
    row = lambda bi, ti: (bi, ti, 0)
    const = lambda bi, ti: (0, 0)
    kern = functools.partial(_outproj_kernel, d_model=d)
    return pl.pallas_call(
        kern,
        grid=(b, s // tm),
        in_specs=[
            pl.BlockSpec((1, tm, d), row),
            pl.BlockSpec((1, tm, DN_WIDTH), row),
            pl.BlockSpec((1, tm, MB_WIDTH), row),
            pl.BlockSpec((1, tm, 2 * d), row),
            pl.BlockSpec((1, 1, 3 * d), lambda bi, ti: (bi, 0, 0)),
            pl.BlockSpec((1, d), const),
            pl.BlockSpec((DN_WIDTH, d), const),
            pl.BlockSpec((MB_WIDTH, d), const),
            pl.BlockSpec((d, d), const),
        ],
        out_specs=pl.BlockSpec((1, tm, d), row),
        out_shape=jax.ShapeDtypeStruct((b, s, d), F32),
        compiler_params=pltpu.CompilerParams(
            dimension_semantics=("parallel", "parallel"),
            vmem_limit_bytes=V7X_VMEM_LIMIT_BYTES),
        name="out_projection",
    )(x, o_dn, o_mb, merge, mod, g_post, w_dn, w_mb, w_out)


def _split_w_in(w_in, d_model):
    sizes = (3 * DN_WIDTH, DN_WIDTH, DN_HEADS, DN_HEADS, 3 * MB_WIDTH, MB_WIDTH, 2 * d_model)
    cuts = [0]
    for n in sizes:
        cuts.append(cuts[-1] + n)
    w_qkv_dn, w_z_dn, w_beta, w_a, w_qkv_mb, w_z_mb, w_merge = (
        w_in[:, cuts[i]:cuts[i + 1]] for i in range(len(sizes)))
    w_ba = jnp.concatenate([w_beta, w_a], axis=1)
    w_ba_pad = jnp.pad(w_ba, ((0, 0), (0, LANES - 2 * DN_HEADS)))
    w_q_mb = w_qkv_mb[:, 0:MB_WIDTH] * (MB_HEAD_DIM ** -0.5 * LOG2_E)
    w_k_mb = w_qkv_mb[:, MB_WIDTH:2 * MB_WIDTH]
    w_tok = jnp.concatenate(
        [w_qkv_dn, w_z_dn, w_q_mb, w_k_mb, w_z_mb, w_merge, w_ba_pad], axis=1).astype(BF16)
    w_vt = w_qkv_mb[:, 2 * MB_WIDTH:3 * MB_WIDTH].T.astype(BF16)
    w_bat = w_ba.T.astype(BF16)
    return w_tok, w_vt, w_bat


def _layer(x, mod, g_pre, g_post, w_in, conv_w, a_log, dt_bias, dn_norm_g, w_proj_dn, w_proj_mb, w_out):
    b, s, d = x.shape
    w_tok, w_vt, w_bat = _split_w_in(w_in, d)
    qkv_dn, z_dn, qk_mb, z_mb, merge, ba, vt, bat = _in_projection(
        x, mod, g_pre.reshape(1, d), w_tok, w_vt, w_bat)

    zeros_h = jnp.zeros((DN_HEADS,), F32)
    a_vec = jnp.concatenate([zeros_h, a_log.astype(F32)])
    dt_vec = jnp.concatenate([zeros_h, dt_bias.astype(F32)])
    p_col = jnp.pad(jnp.stack([a_vec, dt_vec]), ((0, 0), (0, LANES - 2 * DN_HEADS)))
    p_row = jnp.stack([a_vec, dt_vec], axis=1)
    o_dn = _deltanet(qkv_dn, ba, bat, z_dn, conv_w.astype(F32), p_col, p_row,
                     dn_norm_g.reshape(1, DN_HEAD_DIM).astype(F32))
    o_mb = _moba(qk_mb, vt, z_mb)
    return _out_projection(x, o_dn, o_mb, merge, mod, g_post.reshape(1, d),
                           w_proj_dn.astype(BF16), w_proj_mb.astype(BF16), w_out.astype(BF16))


def kernel(x, c, w_ada, b_ada, g_pre, g_post, w_in, conv_w, a_log, dt_bias, dn_norm_g, w_proj_dn, w_proj_mb, w_out):
    depth = w_ada.shape[0]
    b, s, d = x.shape
    assert s % ROW_TILE == 0 and s % MB_BLOCK == 0 and s % DN_TILE == 0
    mod_all = _modulation(c, w_ada, b_ada)
    for l in range(depth):
        mod = mod_all[l].reshape(b, 1, 3 * d)
        x = _layer(x, mod, g_pre[l], g_post[l], w_in[l], conv_w[l], a_log[l], dt_bias[l],
                   dn_norm_g[l], w_proj_dn[l], w_proj_mb[l], w_out[l])
    return x
```

```python
import functools

import jax
import jax.numpy as jnp
from jax import lax
from jax.experimental import pallas as pl
from jax.experimental.pallas import tpu as pltpu

F32 = jnp.float32
BF16 = jnp.bfloat16

NORM_EPS = 1e-6
DN_HEADS = 4
DN_HEAD_DIM = 128
DN_WIDTH = DN_HEADS * DN_HEAD_DIM
DN_CONV = 4
DN_CHUNK = 64
MB_HEADS = 8
MB_HEAD_DIM = 64
MB_WIDTH = MB_HEADS * MB_HEAD_DIM
MB_BLOCK = 256
MB_TOPK = 3

V7X_VMEM_LIMIT_BYTES = 56 * 1024 * 1024
LANES = 128
CONV_HALO = 8

ROW_TILE = 512
CONV_ROWS = 64
DN_TILE = 256
DN_BATCH = 2
KV_GROUP = 3
BF16_SUBLANES = 16
MASK_VALUE = -1e30
LOG2_E = 1.4426950408889634

NT_DIMS = (((1,), (1,)), ((), ()))


def _silu(v):
    return v * jax.nn.sigmoid(v)


def _dot(a, b):
    return jnp.dot(a, b, preferred_element_type=F32)


def _dot_nt(a, b):
    return lax.dot_general(a, b, NT_DIMS, preferred_element_type=F32)


def _split_bf16(v):
    hi = v.astype(BF16)
    lo = (v - hi.astype(F32)).astype(BF16)
    return hi, lo


def _mod_kernel(c_ref, w_ref, b_ref, o_ref):
    sc = _silu(c_ref[...])
    o_ref[0] = jnp.dot(sc, w_ref[0], preferred_element_type=F32,
                       precision=lax.Precision.HIGHEST) + b_ref[0]


def _modulation(c, w_ada, b_ada):
    depth, d, d3 = w_ada.shape
    b = c.shape[0]
    tn = 512
    return pl.pallas_call(
        _mod_kernel,
        grid=(depth, d3 // tn),
        in_specs=[
            pl.BlockSpec((b, d), lambda l, j: (0, 0)),
            pl.BlockSpec((1, d, tn), lambda l, j: (l, 0, j)),
            pl.BlockSpec((1, 1, tn), lambda l, j: (l, 0, j)),
        ],
        out_specs=pl.BlockSpec((1, b, tn), lambda l, j: (l, 0, j)),
        out_shape=jax.ShapeDtypeStruct((depth, b, d3), F32),
        name="adaln_modulation",
    )(c, w_ada, b_ada.reshape(depth, 1, d3))


def _inproj_kernel(x_ref, mod_ref, gpre_ref, wtok_ref, wvt_ref, wbat_ref, convw_ref,
                   qkv_dn_ref, z_dn_ref, qk_mb_ref, z_mb_ref, merge_ref, ba_ref, vt_ref, bat_ref,
                   stage_ref, hist_ref, *, d_model):
    ti = pl.program_id(1)
    tm = x_ref.shape[1]
    dk = DN_HEAD_DIM

    @pl.when(ti == 0)
    def _():
        hist_ref[...] = jnp.zeros(hist_ref.shape, F32)

    x = x_ref[0]
    shift = mod_ref[0, :, 0:d_model]
    scale = mod_ref[0, :, d_model:2 * d_model]
    xn = x * lax.rsqrt(jnp.mean(x * x, axis=-1, keepdims=True) + NORM_EPS)
    h = (xn * gpre_ref[...]) * (1.0 + scale) + shift
    hb = h.astype(BF16)

    def conv_part(part):
        c0 = part * DN_WIDTH
        for hd in range(DN_HEADS):
            l0 = hd * dk
            for t0 in range(0, tm, CONV_ROWS):
                xs = stage_ref[part, t0:t0 + CONV_HALO + CONV_ROWS, l0:l0 + dk]
                acc = convw_ref[0:1, c0 + l0:c0 + l0 + dk] * xs
                for k in range(1, DN_CONV):
                    acc = pltpu.roll(acc, 1, 0) + convw_ref[k:k + 1, c0 + l0:c0 + l0 + dk] * xs
                y = _silu(acc[CONV_HALO:CONV_HALO + CONV_ROWS])
                if part < 2:
                    inv = lax.rsqrt(jnp.sum(y * y, axis=-1, keepdims=True) + NORM_EPS)
                    y = y * (inv * (dk ** -0.5) if part == 0 else inv)
                qkv_dn_ref[0, t0:t0 + CONV_ROWS, c0 + l0:c0 + l0 + dk] = y.astype(qkv_dn_ref.dtype)

    plain = []
    col = 3 * DN_WIDTH
    for ref in (z_dn_ref, qk_mb_ref, z_mb_ref, merge_ref, ba_ref):
        width = ref.shape[-1]
        step = min(width, 512)
        plain += [(ref, c0, step, col + c0) for c0 in range(0, width, step)]
        col += width
    per_part = -(-len(plain) // 3)
    for part in range(3):
        c0 = part * DN_WIDTH
        stage_ref[part, 0:CONV_HALO, :] = hist_ref[part]
        stage_ref[part, CONV_HALO:CONV_HALO + tm, :] = _dot(hb, wtok_ref[:, c0:c0 + DN_WIDTH])
        hist_ref[part] = stage_ref[part, tm:tm + CONV_HALO, :]
        for ref, r0, step, wcol in plain[part * per_part:(part + 1) * per_part]:
            ref[0, :, r0:r0 + step] = _dot(hb, wtok_ref[:, wcol:wcol + step]).astype(ref.dtype)
        conv_part(part)

    vt = _dot_nt(wvt_ref[...], hb).astype(BF16)
    for j in range(vt_ref.shape[1]):
        vt_ref[0, j] = vt[:, j * MB_BLOCK:(j + 1) * MB_BLOCK]
    bat_ref[0] = _dot_nt(wbat_ref[...], hb)


def _in_projection(x, mod, g_pre, w_tok, w_vt, w_bat, conv_w):
    b, s, d = x.shape
    tm = ROW_TILE
    n_tok = w_tok.shape[1]
    kern = functools.partial(_inproj_kernel, d_model=d)
    const = lambda bi, ti: (0, 0)
    row = lambda bi, ti: (bi, ti, 0)
    widths = [3 * DN_WIDTH, DN_WIDTH, 2 * MB_WIDTH, MB_WIDTH, 2 * d, LANES]
    dtypes = [BF16, BF16, BF16, BF16, BF16, F32]
    out_shape = [jax.ShapeDtypeStruct((b, s, w), t) for w, t in zip(widths, dtypes)]
    out_specs = [pl.BlockSpec((1, tm, w), row) for w in widths]
    out_shape += [
        jax.ShapeDtypeStruct((b, s // MB_BLOCK, MB_WIDTH, MB_BLOCK), BF16),
        jax.ShapeDtypeStruct((b, 2 * DN_HEADS, s), F32),
    ]
    out_specs += [
        pl.BlockSpec((1, tm // MB_BLOCK, MB_WIDTH, MB_BLOCK), lambda bi, ti: (bi, ti, 0, 0)),
        pl.BlockSpec((1, 2 * DN_HEADS, tm), lambda bi, ti: (bi, 0, ti)),
    ]
    return pl.pallas_call(
        kern,
        grid=(b, s // tm),
        in_specs=[
            pl.BlockSpec((1, tm, d), row),
            pl.BlockSpec((1, 1, 3 * d), lambda bi, ti: (bi, 0, 0)),
            pl.BlockSpec((1, d), const),
            pl.BlockSpec((d, n_tok), const, pipeline_mode=pl.Buffered(1)),
            pl.BlockSpec((MB_WIDTH, d), const, pipeline_mode=pl.Buffered(1)),
            pl.BlockSpec((2 * DN_HEADS, d), const, pipeline_mode=pl.Buffered(1)),
            pl.BlockSpec((DN_CONV, 3 * DN_WIDTH), const),
        ],
        out_specs=out_specs,
        out_shape=out_shape,
        scratch_shapes=[
            pltpu.VMEM((3, CONV_HALO + tm, DN_WIDTH), F32),
            pltpu.VMEM((3, CONV_HALO, DN_WIDTH), F32),
        ],
        compiler_params=pltpu.CompilerParams(
            dimension_semantics=("parallel", "arbitrary"),
            vmem_limit_bytes=V7X_VMEM_LIMIT_BYTES),
        name="in_projection",
    )(x, mod, g_pre, w_tok, w_vt, w_bat, conv_w)


def _dn_kernel(qkv_ref, ba_ref, bat_ref, z_ref, pcol_ref, prow_ref, gnorm_ref, o_ref, state_ref):
    t = pl.program_id(1)
    nbat, tt = qkv_ref.shape[0], qkv_ref.shape[1]
    n_chunks = tt // DN_CHUNK
    dk = DN_HEAD_DIM
    cs = DN_CHUNK

    @pl.when(t == 0)
    def _():
        state_ref[...] = jnp.zeros(state_ref.shape, F32)

    ri = lax.broadcasted_iota(jnp.int32, (tt, tt), 0)
    ci = lax.broadcasted_iota(jnp.int32, (tt, tt), 1)
    same_chunk = (ri // DN_CHUNK) == (ci // DN_CHUNK)
    lower = (same_chunk & (ri >= ci)).astype(BF16)
    upper = (same_chunk & (ri <= ci)).astype(BF16)

    beta_all, gc_row, eg, eg_to_last, eg_last = [], [], [], [], []
    for bi in range(nbat):
        ba = ba_ref[bi]
        beta_all.append(jax.nn.sigmoid(ba))
        g_col = -jnp.exp(pcol_ref[0:1, :]) * jax.nn.softplus(ba + pcol_ref[1:2, :])
        g_row = -jnp.exp(prow_ref[:, 0:1]) * jax.nn.softplus(bat_ref[bi] + prow_ref[:, 1:2])
        gch, gcl = _split_bf16(g_col)
        gc_col = _dot(lower, gch) + _dot(lower, gcl)
        grh, grl = _split_bf16(g_row)
        gc_row.append(_dot(grh, upper) + _dot(grl, upper))
        eg.append(jnp.exp(gc_col))
        g_last = [gc_col[(c + 1) * DN_CHUNK - 1:(c + 1) * DN_CHUNK, :] for c in range(n_chunks)]
        eg_to_last.append(jnp.concatenate(
            [jnp.exp(g_last[c] - gc_col[c * DN_CHUNK:(c + 1) * DN_CHUNK, :]) for c in range(n_chunks)], axis=0))
        eg_last.append([jnp.exp(g_last[c]) for c in range(n_chunks)])

    col_chunk = lax.broadcasted_iota(jnp.int32, (dk, tt), 1) // cs

    wr = lax.broadcasted_iota(jnp.int32, (cs, tt), 0)
    wl = lax.broadcasted_iota(jnp.int32, (cs, tt), 1)
    wj = wl % cs
    wc = wl // cs
    causal_w = wr >= wj
    strict_w = wr > wj
    same_chunk_b = same_chunk.astype(BF16)

    def fold(x):
        acc = None
        for c in range(n_chunks):
            part = jnp.where(wc == c, x[c * cs:(c + 1) * cs, :], 0.0)
            acc = part if acc is None else acc + part
        return acc

    def block_diag(xw):
        return jnp.where(same_chunk, jnp.concatenate([xw] * n_chunks, axis=0), jnp.zeros((), xw.dtype))

    def blocks_w(n):
        return (wr // n) == (wj // n)

    heads = range(nbat * DN_HEADS)
    bat = [u // DN_HEADS for u in heads]
    hd = [u % DN_HEADS for u in heads]
    gl = [DN_HEADS + hd[h] for h in heads]
    q_b = [qkv_ref[bat[h], :, hd[h] * dk:(hd[h] + 1) * dk] for h in heads]
    kn_b = [qkv_ref[bat[h], :, DN_WIDTH + hd[h] * dk:DN_WIDTH + (hd[h] + 1) * dk] for h in heads]
    qn = [q_b[h].astype(F32) for h in heads]
    kn = [kn_b[h].astype(F32) for h in heads]
    beta = [beta_all[bat[h]][:, hd[h]:hd[h] + 1] for h in heads]
    kb = [kn[h] * beta[h] for h in heads]
    vb = [qkv_ref[bat[h], :, 2 * DN_WIDTH + hd[h] * dk:2 * DN_WIDTH + (hd[h] + 1) * dk].astype(F32) * beta[h]
          for h in heads]
    scores = [_dot_nt(jnp.concatenate([kb[h].astype(BF16), q_b[h]], axis=0), kn_b[h]) for h in heads]
    g_rows = [gc_row[bat[h]][gl[h]:gl[h] + 1, :] for h in heads]
    gcol_w = []
    for h in heads:
        rh, rl = _split_bf16(jnp.where(wj == wr, g_rows[h], 0.0))
        gcol_w.append(_dot(rh, same_chunk_b) + _dot(rl, same_chunk_b))

    a, qk = [], []
    for h in heads:
        diff = gcol_w[h] - g_rows[h]
        decay = jnp.where(causal_w, jnp.exp(jnp.where(causal_w, diff, 0.0)), 0.0)
        a.append(jnp.where(strict_w, fold(scores[h][0:tt]) * decay, 0.0))
        qk.append(fold(scores[h][tt:2 * tt]) * decay)

    p = [(wr == wj).astype(F32) - jnp.where(blocks_w(2), a[h], 0.0) for h in heads]
    n = 2
    while n < cs:
        lower_left = blocks_w(2 * n) & jnp.logical_not(blocks_w(n))
        p_b = [p[h].astype(BF16) for h in heads]
        mid = [_dot(jnp.where(lower_left, a[h], 0.0).astype(BF16), block_diag(p_b[h])) for h in heads]
        upd = [_dot(p_b[h], block_diag(mid[h].astype(BF16))) for h in heads]
        p = [p[h] - upd[h] for h in heads]
        n *= 2
    p = [block_diag(p[h].astype(BF16)) for h in heads]

    eg_h = [eg[bat[h]][:, gl[h]:gl[h] + 1] for h in heads]
    uw = [_dot(p[h], jnp.concatenate([vb[h].astype(BF16), (kb[h] * eg_h[h]).astype(BF16)], axis=1))
          for h in heads]
    u = [uw[h][:, 0:dk] for h in heads]
    w_b = [uw[h][:, dk:2 * dk].astype(BF16) for h in heads]
    qg_b = [(qn[h] * eg_h[h]).astype(BF16) for h in heads]
    kdt = [(kn[h] * eg_to_last[bat[h]][:, gl[h]:gl[h] + 1]).T for h in heads]

    state = [state_ref[h] for h in heads]
    v_rows = [[] for _ in heads]
    o_rows = [[] for _ in heads]
    for c in range(n_chunks):
        rows = slice(c * DN_CHUNK, (c + 1) * DN_CHUNK)
        r2 = [_dot(jnp.concatenate([w_b[h][rows], qg_b[h][rows]], axis=0), state[h].astype(BF16)) for h in heads]
        lhs3 = []
        for h in heads:
            kdt_c = jnp.where(col_chunk == c, kdt[h], 0.0).astype(BF16)
            qk_c = jnp.where(wc == c, qk[h], 0.0).astype(BF16)
            lhs3.append(jnp.concatenate([qk_c, kdt_c], axis=0))
        r3 = []
        for h in heads:
            v_rows[h].append((u[h][rows] - r2[h][0:DN_CHUNK]).astype(BF16))
            v_blk = jnp.concatenate(
                v_rows[h] + [jnp.zeros(((n_chunks - 1 - c) * DN_CHUNK, dk), BF16)] * (c < n_chunks - 1), axis=0)
            r3.append(_dot(lhs3[h], v_blk))
        for h in heads:
            o_rows[h].append(r2[h][DN_CHUNK:2 * DN_CHUNK] + r3[h][0:DN_CHUNK])
            state[h] = state[h] * eg_last[bat[h]][c][:, gl[h]:gl[h] + 1] + r3[h][DN_CHUNK:DN_CHUNK + dk]

    for h in heads:
        state_ref[h] = state[h]
        o = jnp.concatenate(o_rows[h], axis=0)
        on = o * lax.rsqrt(jnp.mean(o * o, axis=-1, keepdims=True) + NORM_EPS) * gnorm_ref[...]
        zh = z_ref[bat[h], :, hd[h] * dk:(hd[h] + 1) * dk].astype(F32)
        o_ref[bat[h], :, hd[h] * dk:(hd[h] + 1) * dk] = (on * _silu(zh)).astype(o_ref.dtype)


def _deltanet(qkv, ba, bat, z, p_col, p_row, g_norm):
    b, s, w3 = qkv.shape
    tt = DN_TILE
    nbat = DN_BATCH if b % DN_BATCH == 0 else 1
    row = lambda bi, ti: (bi, ti, 0)
    const = lambda bi, ti: (0, 0)
    return pl.pallas_call(
        _dn_kernel,
        grid=(b // nbat, s // tt),
        in_specs=[
            pl.BlockSpec((nbat, tt, w3), row),
            pl.BlockSpec((nbat, tt, LANES), row),
            pl.BlockSpec((nbat, 2 * DN_HEADS, tt), lambda bi, ti: (bi, 0, ti)),
            pl.BlockSpec((nbat, tt, DN_WIDTH), row),
            pl.BlockSpec((2, LANES), const),
            pl.BlockSpec((2 * DN_HEADS, 2), const),
            pl.BlockSpec((1, DN_HEAD_DIM), const),
        ],
        out_specs=pl.BlockSpec((nbat, tt, DN_WIDTH), row),
        out_shape=jax.ShapeDtypeStruct((b, s, DN_WIDTH), BF16),
        scratch_shapes=[
            pltpu.VMEM((nbat * DN_HEADS, DN_HEAD_DIM, DN_HEAD_DIM), F32),
        ],
        compiler_params=pltpu.CompilerParams(
            dimension_semantics=("parallel", "arbitrary"),
            vmem_limit_bytes=V7X_VMEM_LIMIT_BYTES),
        name="deltanet",
    )(qkv, ba, bat, z, p_col, p_row, g_norm)


def _moba_kernel(q_ref, k_ref, vt_ref, z_ref, o_ref, kmean_ref, kaug_ref, qaug_ref, s_ref):
    i = pl.program_id(2)
    nb = vt_ref.shape[1]
    blk = MB_BLOCK
    dh = MB_HEAD_DIM
    heads = LANES // dh
    lane = lax.broadcasted_iota(jnp.int32, (1, LANES), 1)
    in_head = [(lane >= h * dh) & (lane < (h + 1) * dh) for h in range(heads)]
    bias_lane0 = [((h + 1) % heads) * dh for h in range(heads)]

    @pl.when(i == 0)
    def _():
        for j in range(nb):
            kj = k_ref[0, j * blk:(j + 1) * blk, :]
            kmean_ref[j:j + 1, :] = jnp.mean(kj.astype(F32), axis=0, keepdims=True)
            for h in range(heads):
                onehot = (lane == bias_lane0[h] + j).astype(BF16)
                kaug_ref[h, j * blk:(j + 1) * blk, :] = jnp.where(in_head[h], kj, onehot)
        for h in range(heads):
            kaug_ref[h, nb * blk:(nb + 1) * blk, :] = jnp.broadcast_to(
                (lane == bias_lane0[h] + nb).astype(BF16), (blk, LANES))
        km = kmean_ref[...]
        pieces = []
        for h in range(heads):
            rest = jnp.where(in_head[h], km, 0.0)
            for _ in range(3):
                part = rest.astype(BF16)
                pieces.append(part)
                rest = rest - part.astype(F32)
        km_parts = jnp.concatenate(pieces, axis=0)
        jidx = lax.broadcasted_iota(jnp.int32, (nb, blk), 0)
        gates = []
        for iq in range(nb):
            g_all = _dot_nt(km_parts, q_ref[0, iq * blk:(iq + 1) * blk, :])
            gates.append([g_all[(3 * h) * nb:(3 * h + 1) * nb] + g_all[(3 * h + 1) * nb:(3 * h + 2) * nb]
                          + g_all[(3 * h + 2) * nb:(3 * h + 3) * nb] for h in range(heads)])
        pad_rows = dh - nb
        for iq in range(nb):
            bias = []
            for h in range(heads):
                gate = jnp.where(jidx < iq, gates[iq][h], -jnp.inf)
                rank = jnp.zeros((nb, blk), jnp.int32)
                for jp in range(iq):
                    other = gate[jp:jp + 1, :]
                    beats = (other > gate) | ((other == gate) & (jp < jidx))
                    rank = rank + beats.astype(jnp.int32)
                keep = ((rank < MB_TOPK) & (jidx < iq)) | (jidx == iq)
                bias.append(jnp.where(keep, 0.0, MASK_VALUE))
            bias_rows = []
            for h in range(heads):
                other = (h + 1) % heads
                bias_rows += [bias[other], jnp.full((1, blk), MASK_VALUE, F32), jnp.zeros((pad_rows - 1, blk), F32)]
            bias_t = jnp.concatenate(bias_rows, axis=0).T.astype(BF16)
            q_blk = q_ref[0, iq * blk:(iq + 1) * blk, :]
            for h in range(heads):
                qaug_ref[h, iq * blk:(iq + 1) * blk, :] = jnp.where(in_head[h], q_blk, bias_t)

    q_aug = [qaug_ref[h, pl.ds(pl.multiple_of(i * blk, blk), blk), :] for h in range(heads)]
    ones_rows = jnp.ones((BF16_SUBLANES, blk), BF16)

    def v_ext(jb, h):
        return jnp.concatenate([vt_ref[0, jb, h * dh:(h + 1) * dh, :], ones_rows], axis=0)

    def score_dots(t):
        tiles = []
        for h in range(heads):
            for g in range(KV_GROUP):
                j = t * KV_GROUP + g
                kb = jnp.where(j < i, j, nb)
                tiles.append(_dot_nt(kaug_ref[h, pl.ds(pl.multiple_of(kb * blk, blk), blk), :], q_aug[h]))
        return tiles

    def stash(tiles):
        maxima = []
        for h in range(heads):
            cm = None
            for g in range(KV_GROUP):
                st = tiles[h * KV_GROUP + g]
                s_ref[h * KV_GROUP + g] = st
                gm = jnp.max(st, axis=0, keepdims=True)
                cm = gm if cm is None else jnp.maximum(cm, gm)
            maxima.append(cm)
        return maxima

    k_own = [kaug_ref[h, pl.ds(pl.multiple_of(i * blk, blk), blk), :] for h in range(heads)]
    key_pos = lax.broadcasted_iota(jnp.int32, (blk, blk), 0)
    qry_pos = lax.broadcasted_iota(jnp.int32, (blk, blk), 1)
    sts = [_dot_nt(k_own[h], q_aug[h]) for h in range(heads)]
    tiles0 = score_dots(0)
    ms, ps = [], []
    for h in range(heads):
        st = jnp.where(key_pos <= qry_pos, sts[h], MASK_VALUE)
        m = jnp.max(st, axis=0, keepdims=True)
        ms.append(m)
        ps.append(jnp.exp2(st - m).astype(BF16))
    cms = stash(tiles0)
    carry = []
    for h in range(heads):
        carry += [ms[h], _dot(v_ext(i, h), ps[h]), cms[h]]

    def body(t, carry):
        tiles_next = score_dots(t + 1)
        new_m, alphas, ps = [], [], []
        for h in range(heads):
            m, _, cm = carry[3 * h:3 * h + 3]
            m_new = jnp.maximum(m, cm)
            new_m.append(m_new)
            alphas.append(jnp.exp2(m - m_new))
            ps.append([jnp.exp2(s_ref[h * KV_GROUP + g] - m_new).astype(BF16) for g in range(KV_GROUP)])
        cm_next = stash(tiles_next)
        out = []
        for h in range(heads):
            pv = None
            for g in range(KV_GROUP):
                vb = jnp.minimum(t * KV_GROUP + g, nb - 1)
                term = _dot(v_ext(vb, h), ps[h][g])
                pv = term if pv is None else pv + term
            out += [new_m[h], alphas[h] * carry[3 * h + 1] + pv, cm_next[h]]
        return tuple(out)

    carry = lax.fori_loop(0, (i + KV_GROUP - 1) // KV_GROUP, body, tuple(carry))
    ot = jnp.concatenate([carry[3 * h + 1][0:dh] / carry[3 * h + 1][dh:dh + 1] for h in range(heads)], axis=0)
    o = ot.T
    o_ref[0] = (o * _silu(z_ref[0].astype(F32))).astype(o_ref.dtype)


def _moba(qk, vt, z):
    b, s, _ = qk.shape
    nb = s // MB_BLOCK
    pairs = MB_WIDTH // LANES
    heads = LANES // MB_HEAD_DIM
    assert nb + 1 <= MB_HEAD_DIM
    return pl.pallas_call(
        _moba_kernel,
        grid=(b, pairs, nb),
        in_specs=[
            pl.BlockSpec((1, s, LANES), lambda bi, hp, i: (bi, 0, hp)),
            pl.BlockSpec((1, s, LANES), lambda bi, hp, i: (bi, 0, pairs + hp)),
            pl.BlockSpec((1, nb, LANES, MB_BLOCK), lambda bi, hp, i: (bi, 0, hp, 0)),
            pl.BlockSpec((1, MB_BLOCK, LANES), lambda bi, hp, i: (bi, i, hp)),
        ],
        out_specs=pl.BlockSpec((1, MB_BLOCK, LANES), lambda bi, hp, i: (bi, i, hp)),
        out_shape=jax.ShapeDtypeStruct((b, s, MB_WIDTH), BF16),
        scratch_shapes=[
            pltpu.VMEM((nb, LANES), F32),
            pltpu.VMEM((heads, s + MB_BLOCK, LANES), BF16),
            pltpu.VMEM((heads, s, LANES), BF16),
            pltpu.VMEM((heads * KV_GROUP, MB_BLOCK, MB_BLOCK), F32),
        ],
        compiler_params=pltpu.CompilerParams(
            dimension_semantics=("parallel", "parallel", "arbitrary"),
            vmem_limit_bytes=V7X_VMEM_LIMIT_BYTES),
        name="moba_attention",
    )(qk, qk, vt, z)


def _outproj_kernel(x_ref, odn_ref, omb_ref, merge_ref, mod_ref, gpost_ref,
                    wdn_ref, wmb_ref, wout_ref, o_ref, *, d_model):
    y_dn = _dot(odn_ref[0], wdn_ref[...])
    y_mb = _dot(omb_ref[0], wmb_ref[...])
    gates = jax.nn.sigmoid(merge_ref[0].astype(F32))
    mixed_in = gates[:, 0:d_model] * y_dn + gates[:, d_model:2 * d_model] * y_mb
    mixed = _dot(mixed_in.astype(BF16), wout_ref[...])
    y = mixed * lax.rsqrt(jnp.mean(mixed * mixed, axis=-1, keepdims=True) + NORM_EPS) * gpost_ref[...]
    gate = mod_ref[0, :, 2 * d_model:3 * d_model]
    o_ref[0] = x_ref[0] + gate * y


def _out_projection(x, o_dn, o_mb, merge, mod, g_post, w_dn, w_mb, w_out):
    b, s, d = x.shape
    tm = ROW_TILE
    row = lambda bi, ti: (bi, ti, 0)
    const = lambda bi, ti: (0, 0)
    kern = functools.partial(_outproj_kernel, d_model=d)
    return pl.pallas_call(
        kern,
        grid=(b, s // tm),
        in_specs=[
            pl.BlockSpec((1, tm, d), row),
            pl.BlockSpec((1, tm, DN_WIDTH), row),
            pl.BlockSpec((1, tm, MB_WIDTH), row),
            pl.BlockSpec((1, tm, 2 * d), row),
            pl.BlockSpec((1, 1, 3 * d), lambda bi, ti: (bi, 0, 0)),
            pl.BlockSpec((1, d), const),
            pl.BlockSpec((DN_WIDTH, d), const),
            pl.BlockSpec((MB_WIDTH, d), const),
            pl.BlockSpec((d, d), const),
        ],
        out_specs=pl.BlockSpec((1, tm, d), row),
        out_shape=jax.ShapeDtypeStruct((b, s, d), F32),
        compiler_params=pltpu.CompilerParams(
            dimension_semantics=("parallel", "parallel"),
            vmem_limit_bytes=V7X_VMEM_LIMIT_BYTES),
        name="out_projection",
    )(x, o_dn, o_mb, merge, mod, g_post, w_dn, w_mb, w_out)


def _split_w_in(w_in, d_model):
    sizes = (3 * DN_WIDTH, DN_WIDTH, DN_HEADS, DN_HEADS, 3 * MB_WIDTH, MB_WIDTH, 2 * d_model)
    cuts = [0]
    for n in sizes:
        cuts.append(cuts[-1] + n)
    w_qkv_dn, w_z_dn, w_beta, w_a, w_qkv_mb, w_z_mb, w_merge = (
        w_in[:, cuts[i]:cuts[i + 1]] for i in range(len(sizes)))
    w_ba = jnp.concatenate([w_beta, w_a], axis=1)
    w_ba_pad = jnp.pad(w_ba, ((0, 0), (0, LANES - 2 * DN_HEADS)))
    w_q_mb = w_qkv_mb[:, 0:MB_WIDTH] * (MB_HEAD_DIM ** -0.5 * LOG2_E)
    w_k_mb = w_qkv_mb[:, MB_WIDTH:2 * MB_WIDTH]
    w_tok = jnp.concatenate(
        [w_qkv_dn, w_z_dn, w_q_mb, w_k_mb, w_z_mb, w_merge, w_ba_pad], axis=1).astype(BF16)
    w_vt = w_qkv_mb[:, 2 * MB_WIDTH:3 * MB_WIDTH].T.astype(BF16)
    w_bat = w_ba.T.astype(BF16)
    return w_tok, w_vt, w_bat


def _layer(x, mod, g_pre, g_post, w_in, conv_w, a_log, dt_bias, dn_norm_g, w_proj_dn, w_proj_mb, w_out):
    b, s, d = x.shape
    w_tok, w_vt, w_bat = _split_w_in(w_in, d)
    qkv_dn, z_dn, qk_mb, z_mb, merge, ba, vt, bat = _in_projection(
        x, mod, g_pre.reshape(1, d), w_tok, w_vt, w_bat, conv_w.astype(F32))

    zeros_h = jnp.zeros((DN_HEADS,), F32)
    a_vec = jnp.concatenate([zeros_h, a_log.astype(F32)])
    dt_vec = jnp.concatenate([zeros_h, dt_bias.astype(F32)])
    p_col = jnp.pad(jnp.stack([a_vec, dt_vec]), ((0, 0), (0, LANES - 2 * DN_HEADS)))
    p_row = jnp.stack([a_vec, dt_vec], axis=1)
    o_dn = _deltanet(qkv_dn, ba, bat, z_dn, p_col, p_row, dn_norm_g.reshape(1, DN_HEAD_DIM).astype(F32))
    o_mb = _moba(qk_mb, vt, z_mb)
    return _out_projection(x, o_dn, o_mb, merge, mod, g_post.reshape(1, d),
                           w_proj_dn.astype(BF16), w_proj_mb.astype(BF16), w_out.astype(BF16))


def kernel(x, c, w_ada, b_ada, g_pre, g_post, w_in, conv_w, a_log, dt_bias, dn_norm_g, w_proj_dn, w_proj_mb, w_out):
    depth = w_ada.shape[0]
    b, s, d = x.shape
    assert s % ROW_TILE == 0 and s % MB_BLOCK == 0 and s % DN_TILE == 0
    mod_all = _modulation(c, w_ada, b_ada)
    for l in range(depth):
        mod = mod_all[l].reshape(b, 1, 3 * d)
        x = _layer(x, mod, g_pre[l], g_post[l], w_in[l], conv_w[l], a_log[l], dt_bias[l],
                   dn_norm_g[l], w_proj_dn[l], w_proj_mb[l], w_out[l])
    return x
```

```python
import functools

import jax
import jax.numpy as jnp
from jax import lax
from jax.experimental import pallas as pl
from jax.experimental.pallas import tpu as pltpu

F32 = jnp.float32
BF16 = jnp.bfloat16

NORM_EPS = 1e-6
DN_HEADS = 4
DN_HEAD_DIM = 128
DN_WIDTH = DN_HEADS * DN_HEAD_DIM
DN_CONV = 4
DN_CHUNK = 64
MB_HEADS = 8
MB_HEAD_DIM = 64
MB_WIDTH = MB_HEADS * MB_HEAD_DIM
MB_BLOCK = 256
MB_TOPK = 3

V7X_VMEM_LIMIT_BYTES = 56 * 1024 * 1024
LANES = 128
CONV_HALO = 8

ROW_TILE = 512
CONV_ROWS = 64
DN_TILE = 256
DN_BATCH = 2
KV_GROUP = 3
Q_BLOCKS = 2
BF16_SUBLANES = 16
MASK_VALUE = -1e30
LOG2_E = 1.4426950408889634

NT_DIMS = (((1,), (1,)), ((), ()))


def _silu(v):
    return v * jax.nn.sigmoid(v)


def _dot(a, b):
    return jnp.dot(a, b, preferred_element_type=F32)


def _dot_nt(a, b):
    return lax.dot_general(a, b, NT_DIMS, preferred_element_type=F32)


def _split_bf16(v):
    hi = v.astype(BF16)
    lo = (v - hi.astype(F32)).astype(BF16)
    return hi, lo


def _mod_kernel(c_ref, w_ref, b_ref, o_ref):
    sc = _silu(c_ref[...])
    o_ref[0] = jnp.dot(sc, w_ref[0], preferred_element_type=F32,
                       precision=lax.Precision.HIGHEST) + b_ref[0]


def _modulation(c, w_ada, b_ada):
    depth, d, d3 = w_ada.shape
    b = c.shape[0]
    tn = 512
    return pl.pallas_call(
        _mod_kernel,
        grid=(depth, d3 // tn),
        in_specs=[
            pl.BlockSpec((b, d), lambda l, j: (0, 0)),
            pl.BlockSpec((1, d, tn), lambda l, j: (l, 0, j)),
            pl.BlockSpec((1, 1, tn), lambda l, j: (l, 0, j)),
        ],
        out_specs=pl.BlockSpec((1, b, tn), lambda l, j: (l, 0, j)),
        out_shape=jax.ShapeDtypeStruct((depth, b, d3), F32),
        name="adaln_modulation",
    )(c, w_ada, b_ada.reshape(depth, 1, d3))


def _inproj_kernel(x_ref, mod_ref, gpre_ref, wtok_ref, wvt_ref, wbat_ref, convw_ref,
                   qkv_dn_ref, z_dn_ref, qk_mb_ref, z_mb_ref, merge_ref, ba_ref, vt_ref, bat_ref,
                   stage_ref, hist_ref, *, d_model):
    ti = pl.program_id(1)
    tm = x_ref.shape[1]
    dk = DN_HEAD_DIM

    @pl.when(ti == 0)
    def _():
        hist_ref[...] = jnp.zeros(hist_ref.shape, F32)

    x = x_ref[0]
    shift = mod_ref[0, :, 0:d_model]
    scale = mod_ref[0, :, d_model:2 * d_model]
    xn = x * lax.rsqrt(jnp.mean(x * x, axis=-1, keepdims=True) + NORM_EPS)
    h = (xn * gpre_ref[...]) * (1.0 + scale) + shift
    hb = h.astype(BF16)

    def conv_part(part):
        c0 = part * DN_WIDTH
        for hd in range(DN_HEADS):
            l0 = hd * dk
            for t0 in range(0, tm, CONV_ROWS):
                xs = stage_ref[part, t0:t0 + CONV_HALO + CONV_ROWS, l0:l0 + dk]
                acc = convw_ref[0:1, c0 + l0:c0 + l0 + dk] * xs
                for k in range(1, DN_CONV):
                    acc = pltpu.roll(acc, 1, 0) + convw_ref[k:k + 1, c0 + l0:c0 + l0 + dk] * xs
                y = _silu(acc[CONV_HALO:CONV_HALO + CONV_ROWS])
                if part < 2:
                    inv = lax.rsqrt(jnp.sum(y * y, axis=-1, keepdims=True) + NORM_EPS)
                    y = y * (inv * (dk ** -0.5) if part == 0 else inv)
                qkv_dn_ref[0, t0:t0 + CONV_ROWS, c0 + l0:c0 + l0 + dk] = y.astype(qkv_dn_ref.dtype)

    plain = []
    col = 3 * DN_WIDTH
    for ref in (z_dn_ref, qk_mb_ref, z_mb_ref, merge_ref, ba_ref):
        width = ref.shape[-1]
        step = min(width, 512)
        plain += [(ref, c0, step, col + c0) for c0 in range(0, width, step)]
        col += width
    per_part = -(-len(plain) // 3)
    for part in range(3):
        c0 = part * DN_WIDTH
        stage_ref[part, 0:CONV_HALO, :] = hist_ref[part]
        stage_ref[part, CONV_HALO:CONV_HALO + tm, :] = _dot(hb, wtok_ref[:, c0:c0 + DN_WIDTH])
        hist_ref[part] = stage_ref[part, tm:tm + CONV_HALO, :]
        for ref, r0, step, wcol in plain[part * per_part:(part + 1) * per_part]:
            ref[0, :, r0:r0 + step] = _dot(hb, wtok_ref[:, wcol:wcol + step]).astype(ref.dtype)
        conv_part(part)

    vt = _dot_nt(wvt_ref[...], hb).astype(BF16)
    for j in range(vt_ref.shape[1]):
        vt_ref[0, j] = vt[:, j * MB_BLOCK:(j + 1) * MB_BLOCK]
    bat_ref[0] = _dot_nt(wbat_ref[...], hb)


def _in_projection(x, mod, g_pre, w_tok, w_vt, w_bat, conv_w):
    b, s, d = x.shape
    tm = ROW_TILE
    n_tok = w_tok.shape[1]
    kern = functools.partial(_inproj_kernel, d_model=d)
    const = lambda bi, ti: (0, 0)
    row = lambda bi, ti: (bi, ti, 0)
    widths = [3 * DN_WIDTH, DN_WIDTH, 2 * MB_WIDTH, MB_WIDTH, 2 * d, LANES]
    dtypes = [BF16, BF16, BF16, BF16, BF16, F32]
    out_shape = [jax.ShapeDtypeStruct((b, s, w), t) for w, t in zip(widths, dtypes)]
    out_specs = [pl.BlockSpec((1, tm, w), row) for w in widths]
    out_shape += [
        jax.ShapeDtypeStruct((b, s // MB_BLOCK, MB_WIDTH, MB_BLOCK), BF16),
        jax.ShapeDtypeStruct((b, 2 * DN_HEADS, s), F32),
    ]
    out_specs += [
        pl.BlockSpec((1, tm // MB_BLOCK, MB_WIDTH, MB_BLOCK), lambda bi, ti: (bi, ti, 0, 0)),
        pl.BlockSpec((1, 2 * DN_HEADS, tm), lambda bi, ti: (bi, 0, ti)),
    ]
    return pl.pallas_call(
        kern,
        grid=(b, s // tm),
        in_specs=[
            pl.BlockSpec((1, tm, d), row),
            pl.BlockSpec((1, 1, 3 * d), lambda bi, ti: (bi, 0, 0)),
            pl.BlockSpec((1, d), const),
            pl.BlockSpec((d, n_tok), const, pipeline_mode=pl.Buffered(1)),
            pl.BlockSpec((MB_WIDTH, d), const, pipeline_mode=pl.Buffered(1)),
            pl.BlockSpec((2 * DN_HEADS, d), const, pipeline_mode=pl.Buffered(1)),
            pl.BlockSpec((DN_CONV, 3 * DN_WIDTH), const),
        ],
        out_specs=out_specs,
        out_shape=out_shape,
        scratch_shapes=[
            pltpu.VMEM((3, CONV_HALO + tm, DN_WIDTH), F32),
            pltpu.VMEM((3, CONV_HALO, DN_WIDTH), F32),
        ],
        compiler_params=pltpu.CompilerParams(
            dimension_semantics=("parallel", "arbitrary"),
            vmem_limit_bytes=V7X_VMEM_LIMIT_BYTES),
        name="in_projection",
    )(x, mod, g_pre, w_tok, w_vt, w_bat, conv_w)


def _dn_kernel(qkv_ref, ba_ref, bat_ref, z_ref, pcol_ref, prow_ref, gnorm_ref, o_ref, state_ref):
    t = pl.program_id(1)
    nbat, tt = qkv_ref.shape[0], qkv_ref.shape[1]
    n_chunks = tt // DN_CHUNK
    dk = DN_HEAD_DIM
    cs = DN_CHUNK

    @pl.when(t == 0)
    def _():
        state_ref[...] = jnp.zeros(state_ref.shape, F32)

    ri = lax.broadcasted_iota(jnp.int32, (tt, tt), 0)
    ci = lax.broadcasted_iota(jnp.int32, (tt, tt), 1)
    same_chunk = (ri // DN_CHUNK) == (ci // DN_CHUNK)
    lower = (same_chunk & (ri >= ci)).astype(BF16)
    upper = (same_chunk & (ri <= ci)).astype(BF16)

    beta_all, gc_row, eg, eg_to_last, eg_last = [], [], [], [], []
    for bi in range(nbat):
        ba = ba_ref[bi]
        beta_all.append(jax.nn.sigmoid(ba))
        g_col = -jnp.exp(pcol_ref[0:1, :]) * jax.nn.softplus(ba + pcol_ref[1:2, :])
        g_row = -jnp.exp(prow_ref[:, 0:1]) * jax.nn.softplus(bat_ref[bi] + prow_ref[:, 1:2])
        gch, gcl = _split_bf16(g_col)
        gc_col = _dot(lower, gch) + _dot(lower, gcl)
        grh, grl = _split_bf16(g_row)
        gc_row.append(_dot(grh, upper) + _dot(grl, upper))
        eg.append(jnp.exp(gc_col))
        g_last = [gc_col[(c + 1) * DN_CHUNK - 1:(c + 1) * DN_CHUNK, :] for c in range(n_chunks)]
        eg_to_last.append(jnp.concatenate(
            [jnp.exp(g_last[c] - gc_col[c * DN_CHUNK:(c + 1) * DN_CHUNK, :]) for c in range(n_chunks)], axis=0))
        eg_last.append([jnp.exp(g_last[c]) for c in range(n_chunks)])

    col_chunk = lax.broadcasted_iota(jnp.int32, (dk, tt), 1) // cs

    wr = lax.broadcasted_iota(jnp.int32, (cs, tt), 0)
    wl = lax.broadcasted_iota(jnp.int32, (cs, tt), 1)
    wj = wl % cs
    wc = wl // cs
    causal_w = wr >= wj
    strict_w = wr > wj
    same_chunk_b = same_chunk.astype(BF16)

    def fold(x):
        acc = None
        for c in range(n_chunks):
            part = jnp.where(wc == c, x[c * cs:(c + 1) * cs, :], 0.0)
            acc = part if acc is None else acc + part
        return acc

    def block_diag(xw):
        return jnp.where(same_chunk, jnp.concatenate([xw] * n_chunks, axis=0), jnp.zeros((), xw.dtype))

    def blocks_w(n):
        return (wr // n) == (wj // n)

    heads = range(nbat * DN_HEADS)
    bat = [u // DN_HEADS for u in heads]
    hd = [u % DN_HEADS for u in heads]
    gl = [DN_HEADS + hd[h] for h in heads]
    q_b = [qkv_ref[bat[h], :, hd[h] * dk:(hd[h] + 1) * dk] for h in heads]
    kn_b = [qkv_ref[bat[h], :, DN_WIDTH + hd[h] * dk:DN_WIDTH + (hd[h] + 1) * dk] for h in heads]
    qn = [q_b[h].astype(F32) for h in heads]
    kn = [kn_b[h].astype(F32) for h in heads]
    beta = [beta_all[bat[h]][:, hd[h]:hd[h] + 1] for h in heads]
    kb = [kn[h] * beta[h] for h in heads]
    vb = [qkv_ref[bat[h], :, 2 * DN_WIDTH + hd[h] * dk:2 * DN_WIDTH + (hd[h] + 1) * dk].astype(F32) * beta[h]
          for h in heads]
    scores = [_dot_nt(jnp.concatenate([kb[h].astype(BF16), q_b[h]], axis=0), kn_b[h]) for h in heads]
    g_rows = [gc_row[bat[h]][gl[h]:gl[h] + 1, :] for h in heads]
    gcol_w = []
    for h in heads:
        rh, rl = _split_bf16(jnp.where(wj == wr, g_rows[h], 0.0))
        gcol_w.append(_dot(rh, same_chunk_b) + _dot(rl, same_chunk_b))

    a, qk = [], []
    for h in heads:
        diff = gcol_w[h] - g_rows[h]
        decay = jnp.where(causal_w, jnp.exp(jnp.where(causal_w, diff, 0.0)), 0.0)
        a.append(jnp.where(strict_w, fold(scores[h][0:tt]) * decay, 0.0))
        qk.append(fold(scores[h][tt:2 * tt]) * decay)

    p = [(wr == wj).astype(F32) - jnp.where(blocks_w(2), a[h], 0.0) for h in heads]
    n = 2
    while n < cs:
        lower_left = blocks_w(2 * n) & jnp.logical_not(blocks_w(n))
        p_b = [p[h].astype(BF16) for h in heads]
        mid = [_dot(jnp.where(lower_left, a[h], 0.0).astype(BF16), block_diag(p_b[h])) for h in heads]
        upd = [_dot(p_b[h], block_diag(mid[h].astype(BF16))) for h in heads]
        p = [p[h] - upd[h] for h in heads]
        n *= 2
    p = [block_diag(p[h].astype(BF16)) for h in heads]

    eg_h = [eg[bat[h]][:, gl[h]:gl[h] + 1] for h in heads]
    uw = [_dot(p[h], jnp.concatenate([vb[h].astype(BF16), (kb[h] * eg_h[h]).astype(BF16)], axis=1))
          for h in heads]
    u = [uw[h][:, 0:dk] for h in heads]
    w_b = [uw[h][:, dk:2 * dk].astype(BF16) for h in heads]
    qg_b = [(qn[h] * eg_h[h]).astype(BF16) for h in heads]
    kdt = [(kn[h] * eg_to_last[bat[h]][:, gl[h]:gl[h] + 1]).T for h in heads]

    state = [state_ref[h] for h in heads]
    v_rows = [[] for _ in heads]
    o_rows = [[] for _ in heads]
    for c in range(n_chunks):
        rows = slice(c * DN_CHUNK, (c + 1) * DN_CHUNK)
        r2 = [_dot(jnp.concatenate([w_b[h][rows], qg_b[h][rows]], axis=0), state[h].astype(BF16)) for h in heads]
        lhs3 = []
        for h in heads:
            kdt_c = jnp.where(col_chunk == c, kdt[h], 0.0).astype(BF16)
            qk_c = jnp.where(wc == c, qk[h], 0.0).astype(BF16)
            lhs3.append(jnp.concatenate([qk_c, kdt_c], axis=0))
        r3 = []
        for h in heads:
            v_rows[h].append((u[h][rows] - r2[h][0:DN_CHUNK]).astype(BF16))
            v_blk = jnp.concatenate(
                v_rows[h] + [jnp.zeros(((n_chunks - 1 - c) * DN_CHUNK, dk), BF16)] * (c < n_chunks - 1), axis=0)
            r3.append(_dot(lhs3[h], v_blk))
        for h in heads:
            o_rows[h].append(r2[h][DN_CHUNK:2 * DN_CHUNK] + r3[h][0:DN_CHUNK])
            state[h] = state[h] * eg_last[bat[h]][c][:, gl[h]:gl[h] + 1] + r3[h][DN_CHUNK:DN_CHUNK + dk]

    for h in heads:
        state_ref[h] = state[h]
        o = jnp.concatenate(o_rows[h], axis=0)
        on = o * lax.rsqrt(jnp.mean(o * o, axis=-1, keepdims=True) + NORM_EPS) * gnorm_ref[...]
        zh = z_ref[bat[h], :, hd[h] * dk:(hd[h] + 1) * dk].astype(F32)
        o_ref[bat[h], :, hd[h] * dk:(hd[h] + 1) * dk] = (on * _silu(zh)).astype(o_ref.dtype)


def _deltanet(qkv, ba, bat, z, p_col, p_row, g_norm):
    b, s, w3 = qkv.shape
    tt = DN_TILE
    nbat = DN_BATCH if b % DN_BATCH == 0 else 1
    row = lambda bi, ti: (bi, ti, 0)
    const = lambda bi, ti: (0, 0)
    return pl.pallas_call(
        _dn_kernel,
        grid=(b // nbat, s // tt),
        in_specs=[
            pl.BlockSpec((nbat, tt, w3), row),
            pl.BlockSpec((nbat, tt, LANES), row),
            pl.BlockSpec((nbat, 2 * DN_HEADS, tt), lambda bi, ti: (bi, 0, ti)),
            pl.BlockSpec((nbat, tt, DN_WIDTH), row),
            pl.BlockSpec((2, LANES), const),
            pl.BlockSpec((2 * DN_HEADS, 2), const),
            pl.BlockSpec((1, DN_HEAD_DIM), const),
        ],
        out_specs=pl.BlockSpec((nbat, tt, DN_WIDTH), row),
        out_shape=jax.ShapeDtypeStruct((b, s, DN_WIDTH), BF16),
        scratch_shapes=[
            pltpu.VMEM((nbat * DN_HEADS, DN_HEAD_DIM, DN_HEAD_DIM), F32),
        ],
        compiler_params=pltpu.CompilerParams(
            dimension_semantics=("parallel", "arbitrary"),
            vmem_limit_bytes=V7X_VMEM_LIMIT_BYTES),
        name="deltanet",
    )(qkv, ba, bat, z, p_col, p_row, g_norm)


def _moba_kernel(q_ref, k_ref, vt_ref, z_ref, o_ref, kmean_ref, kaug_ref, qaug_ref, s_ref):
    a = pl.program_id(2)
    i = a * Q_BLOCKS
    nb = vt_ref.shape[1]
    blk = MB_BLOCK
    qw = Q_BLOCKS * blk
    dh = MB_HEAD_DIM
    heads = LANES // dh
    lane = lax.broadcasted_iota(jnp.int32, (1, LANES), 1)
    in_head = [(lane >= h * dh) & (lane < (h + 1) * dh) for h in range(heads)]
    bias_lane0 = [((h + 1) % heads) * dh for h in range(heads)]

    @pl.when(a == 0)
    def _():
        for j in range(nb):
            kj = k_ref[0, j * blk:(j + 1) * blk, :]
            kmean_ref[j:j + 1, :] = jnp.mean(kj.astype(F32), axis=0, keepdims=True)
            for h in range(heads):
                onehot = (lane == bias_lane0[h] + j).astype(BF16)
                kaug_ref[h, j * blk:(j + 1) * blk, :] = jnp.where(in_head[h], kj, onehot)
        for h in range(heads):
            kaug_ref[h, nb * blk:(nb + 1) * blk, :] = jnp.broadcast_to(
                (lane == bias_lane0[h] + nb).astype(BF16), (blk, LANES))
        km = kmean_ref[...]
        pieces = []
        for h in range(heads):
            rest = jnp.where(in_head[h], km, 0.0)
            for _ in range(3):
                part = rest.astype(BF16)
                pieces.append(part)
                rest = rest - part.astype(F32)
        km_parts = jnp.concatenate(pieces, axis=0)
        jidx = lax.broadcasted_iota(jnp.int32, (nb, blk), 0)
        gates = []
        for iq in range(nb):
            g_all = _dot_nt(km_parts, q_ref[0, iq * blk:(iq + 1) * blk, :])
            gates.append([g_all[(3 * h) * nb:(3 * h + 1) * nb] + g_all[(3 * h + 1) * nb:(3 * h + 2) * nb]
                          + g_all[(3 * h + 2) * nb:(3 * h + 3) * nb] for h in range(heads)])
        pad_rows = dh - nb
        for iq in range(nb):
            bias = []
            for h in range(heads):
                gate = jnp.where(jidx < iq, gates[iq][h], -jnp.inf)
                rank = jnp.zeros((nb, blk), jnp.int32)
                for jp in range(iq):
                    other = gate[jp:jp + 1, :]
                    beats = (other > gate) | ((other == gate) & (jp < jidx))
                    rank = rank + beats.astype(jnp.int32)
                keep = (rank < MB_TOPK) & (jidx < iq)
                bias.append(jnp.where(keep, 0.0, MASK_VALUE))
            bias_rows = []
            for h in range(heads):
                other = (h + 1) % heads
                bias_rows += [bias[other], jnp.full((1, blk), MASK_VALUE, F32), jnp.zeros((pad_rows - 1, blk), F32)]
            bias_t = jnp.concatenate(bias_rows, axis=0).T.astype(BF16)
            q_blk = q_ref[0, iq * blk:(iq + 1) * blk, :]
            for h in range(heads):
                qaug_ref[h, iq * blk:(iq + 1) * blk, :] = jnp.where(in_head[h], q_blk, bias_t)

    q_aug = [qaug_ref[h, pl.ds(pl.multiple_of(a * qw, qw), qw), :] for h in range(heads)]
    ones_rows = jnp.ones((BF16_SUBLANES, blk), BF16)

    def v_ext(jb, h):
        return jnp.concatenate([vt_ref[0, jb, h * dh:(h + 1) * dh, :], ones_rows], axis=0)

    def score_dots(t):
        tiles = []
        for h in range(heads):
            for g in range(KV_GROUP):
                j = t * KV_GROUP + g
                kb = jnp.where(j < i, j, nb)
                tiles.append(_dot_nt(kaug_ref[h, pl.ds(pl.multiple_of(kb * blk, blk), blk), :], q_aug[h]))
        return tiles

    def stash(tiles):
        maxima = []
        for h in range(heads):
            cm = None
            for g in range(KV_GROUP):
                st = tiles[h * KV_GROUP + g]
                s_ref[h * KV_GROUP + g] = st
                gm = jnp.max(st, axis=0, keepdims=True)
                cm = gm if cm is None else jnp.maximum(cm, gm)
            maxima.append(cm)
        return maxima

    key_pos = lax.broadcasted_iota(jnp.int32, (blk, qw), 0)
    qry_pos = lax.broadcasted_iota(jnp.int32, (blk, qw), 1)
    q_row = lax.broadcasted_iota(jnp.int32, (qw, 1), 0)
    own_tiles = []
    for h in range(heads):
        plain = jnp.where(in_head[h], q_aug[h], jnp.zeros_like(q_aug[h]))
        for d in range(Q_BLOCKS):
            kd = kaug_ref[h, pl.ds(pl.multiple_of((i + d) * blk, blk), blk), :]
            own_tiles.append(_dot_nt(kd, jnp.where(q_row < (d + 1) * blk, plain, q_aug[h])))
    tiles0 = score_dots(0)
    ms, ps = [], []
    for h in range(heads):
        st = []
        for d in range(Q_BLOCKS):
            rel = qry_pos - d * blk
            visible = (rel >= blk) | ((rel >= 0) & (key_pos <= rel))
            st.append(jnp.where(visible, own_tiles[h * Q_BLOCKS + d], MASK_VALUE))
        m = jnp.max(st[0], axis=0, keepdims=True)
        for d in range(1, Q_BLOCKS):
            m = jnp.maximum(m, jnp.max(st[d], axis=0, keepdims=True))
        ms.append(m)
        ps.append([jnp.exp2(st[d] - m).astype(BF16) for d in range(Q_BLOCKS)])
    cms = stash(tiles0)
    carry = []
    for h in range(heads):
        acc = None
        for d in range(Q_BLOCKS):
            term = _dot(v_ext(i + d, h), ps[h][d])
            acc = term if acc is None else acc + term
        carry += [ms[h], acc, cms[h]]

    def body(t, carry):
        tiles_next = score_dots(t + 1)
        new_m, alphas, ps = [], [], []
        for h in range(heads):
            m, _, cm = carry[3 * h:3 * h + 3]
            m_new = jnp.maximum(m, cm)
            new_m.append(m_new)
            alphas.append(jnp.exp2(m - m_new))
            ps.append([jnp.exp2(s_ref[h * KV_GROUP + g] - m_new).astype(BF16) for g in range(KV_GROUP)])
        cm_next = stash(tiles_next)
        out = []
        for h in range(heads):
            pv = None
            for g in range(KV_GROUP):
                vb = jnp.minimum(t * KV_GROUP + g, nb - 1)
                term = _dot(v_ext(vb, h), ps[h][g])
                pv = term if pv is None else pv + term
            out += [new_m[h], alphas[h] * carry[3 * h + 1] + pv, cm_next[h]]
        return tuple(out)

    carry = lax.fori_loop(0, (i + KV_GROUP - 1) // KV_GROUP, body, tuple(carry))
    ot = jnp.concatenate([carry[3 * h + 1][0:dh] / carry[3 * h + 1][dh:dh + 1] for h in range(heads)], axis=0)
    o = ot.T
    o_ref[0] = (o * _silu(z_ref[0].astype(F32))).astype(o_ref.dtype)


def _moba(qk, vt, z):
    b, s, _ = qk.shape
    nb = s // MB_BLOCK
    pairs = MB_WIDTH // LANES
    heads = LANES // MB_HEAD_DIM
    assert nb + 1 <= MB_HEAD_DIM
    assert nb % Q_BLOCKS == 0
    q_rows = Q_BLOCKS * MB_BLOCK
    return pl.pallas_call(
        _moba_kernel,
        grid=(b, pairs, nb // Q_BLOCKS),
        in_specs=[
            pl.BlockSpec((1, s, LANES), lambda bi, hp, i: (bi, 0, hp)),
            pl.BlockSpec((1, s, LANES), lambda bi, hp, i: (bi, 0, pairs + hp)),
            pl.BlockSpec((1, nb, LANES, MB_BLOCK), lambda bi, hp, i: (bi, 0, hp, 0)),
            pl.BlockSpec((1, q_rows, LANES), lambda bi, hp, i: (bi, i, hp)),
        ],
        out_specs=pl.BlockSpec((1, q_rows, LANES), lambda bi, hp, i: (bi, i, hp)),
        out_shape=jax.ShapeDtypeStruct((b, s, MB_WIDTH), BF16),
        scratch_shapes=[
            pltpu.VMEM((nb, LANES), F32),
            pltpu.VMEM((heads, s + MB_BLOCK, LANES), BF16),
            pltpu.VMEM((heads, s, LANES), BF16),
            pltpu.VMEM((heads * KV_GROUP, MB_BLOCK, q_rows), F32),
        ],
        compiler_params=pltpu.CompilerParams(
            dimension_semantics=("parallel", "parallel", "arbitrary"),
            vmem_limit_bytes=V7X_VMEM_LIMIT_BYTES),
        name="moba_attention",
    )(qk, qk, vt, z)


def _outproj_kernel(x_ref, odn_ref, omb_ref, merge_ref, mod_ref, gpost_ref,
                    wdn_ref, wmb_ref, wout_ref, o_ref, *, d_model):
    y_dn = _dot(odn_ref[0], wdn_ref[...])
    y_mb = _dot(omb_ref[0], wmb_ref[...])
    gates = jax.nn.sigmoid(merge_ref[0].astype(F32))
    mixed_in = gates[:, 0:d_model] * y_dn + gates[:, d_model:2 * d_model] * y_mb
    mixed = _dot(mixed_in.astype(BF16), wout_ref[...])
    y = mixed * lax.rsqrt(jnp.mean(mixed * mixed, axis=-1, keepdims=True) + NORM_EPS) * gpost_ref[...]
    gate = mod_ref[0, :, 2 * d_model:3 * d_model]
    o_ref[0] = x_ref[0] + gate * y


def _out_projection(x, o_dn, o_mb, merge, mod, g_post, w_dn, w_mb, w_out):
    b, s, d = x.shape
    tm = ROW_TILE
    row = lambda bi, ti: (bi, ti, 0)
    const = lambda bi, ti: (0, 0)
    kern = functools.partial(_outproj_kernel, d_model=d)
    return pl.pallas_call(
        kern,
        grid=(b, s // tm),
        in_specs=[
            pl.BlockSpec((1, tm, d), row),
            pl.BlockSpec((1, tm, DN_WIDTH), row),
            pl.BlockSpec((1, tm, MB_WIDTH), row),
            pl.BlockSpec((1, tm, 2 * d), row),
            pl.BlockSpec((1, 1, 3 * d), lambda bi, ti: (bi, 0, 0)),
            pl.BlockSpec((1, d), const),
            pl.BlockSpec((DN_WIDTH, d), const),
            pl.BlockSpec((MB_WIDTH, d), const),
            pl.BlockSpec((d, d), const),
        ],
        out_specs=pl.BlockSpec((1, tm, d), row),
        out_shape=jax.ShapeDtypeStruct((b, s, d), F32),
        compiler_params=pltpu.CompilerParams(
            dimension_semantics=("parallel", "parallel"),
            vmem_limit_bytes=V7X_VMEM_LIMIT_BYTES),
        name="out_projection",
    )(x, o_dn, o_mb, merge, mod, g_post, w_dn, w_mb, w_out)


def _split_w_in(w_in, d_model):
    sizes = (3 * DN_WIDTH, DN_WIDTH, DN_HEADS, DN_HEADS, 3 * MB_WIDTH, MB_WIDTH, 2 * d_model)
    cuts = [0]
    for n in sizes:
        cuts.append(cuts[-1] + n)
    w_qkv_dn, w_z_dn, w_beta, w_a, w_qkv_mb, w_z_mb, w_merge = (
        w_in[:, cuts[i]:cuts[i + 1]] for i in range(len(sizes)))
    w_ba = jnp.concatenate([w_beta, w_a], axis=1)
    w_ba_pad = jnp.pad(w_ba, ((0, 0), (0, LANES - 2 * DN_HEADS)))
    w_q_mb = w_qkv_mb[:, 0:MB_WIDTH] * (MB_HEAD_DIM ** -0.5 * LOG2_E)
    w_k_mb = w_qkv_mb[:, MB_WIDTH:2 * MB_WIDTH]
    w_tok = jnp.concatenate(
        [w_qkv_dn, w_z_dn, w_q_mb, w_k_mb, w_z_mb, w_merge, w_ba_pad], axis=1).astype(BF16)
    w_vt = w_qkv_mb[:, 2 * MB_WIDTH:3 * MB_WIDTH].T.astype(BF16)
    w_bat = w_ba.T.astype(BF16)
    return w_tok, w_vt, w_bat


def _layer(x, mod, g_pre, g_post, w_in, conv_w, a_log, dt_bias, dn_norm_g, w_proj_dn, w_proj_mb, w_out):
    b, s, d = x.shape
    w_tok, w_vt, w_bat = _split_w_in(w_in, d)
    qkv_dn, z_dn, qk_mb, z_mb, merge, ba, vt, bat = _in_projection(
        x, mod, g_pre.reshape(1, d), w_tok, w_vt, w_bat, conv_w.astype(F32))

    zeros_h = jnp.zeros((DN_HEADS,), F32)
    a_vec = jnp.concatenate([zeros_h, a_log.astype(F32)])
    dt_vec = jnp.concatenate([zeros_h, dt_bias.astype(F32)])
    p_col = jnp.pad(jnp.stack([a_vec, dt_vec]), ((0, 0), (0, LANES - 2 * DN_HEADS)))
    p_row = jnp.stack([a_vec, dt_vec], axis=1)
    o_dn = _deltanet(qkv_dn, ba, bat, z_dn, p_col, p_row, dn_norm_g.reshape(1, DN_HEAD_DIM).astype(F32))
    o_mb = _moba(qk_mb, vt, z_mb)
    return _out_projection(x, o_dn, o_mb, merge, mod, g_post.reshape(1, d),
                           w_proj_dn.astype(BF16), w_proj_mb.astype(BF16), w_out.astype(BF16))


def kernel(x, c, w_ada, b_ada, g_pre, g_post, w_in, conv_w, a_log, dt_bias, dn_norm_g, w_proj_dn, w_proj_mb, w_out):
    depth = w_ada.shape[0]
    b, s, d = x.shape
    assert s % ROW_TILE == 0 and s % MB_BLOCK == 0 and s % DN_TILE == 0
    mod_all = _modulation(c, w_ada, b_ada)
    for l in range(depth):
        mod = mod_all[l].reshape(b, 1, 3 * d)
        x = _layer(x, mod, g_pre[l], g_post[l], w_in[l], conv_w[l], a_log[l], dt_bias[l],
                   dn_norm_g[l], w_proj_dn[l], w_proj_mb[l], w_out[l])
    return x
```

```python
import functools

import jax
import jax.numpy as jnp
from jax import lax
from jax.experimental import pallas as pl
from jax.experimental.pallas import tpu as pltpu

F32 = jnp.float32
BF16 = jnp.bfloat16

NORM_EPS = 1e-6
DN_HEADS = 4
DN_HEAD_DIM = 128
DN_WIDTH = DN_HEADS * DN_HEAD_DIM
DN_CONV = 4
DN_CHUNK = 64
MB_HEADS = 8
MB_HEAD_DIM = 64
MB_WIDTH = MB_HEADS * MB_HEAD_DIM
MB_BLOCK = 256
MB_TOPK = 3

V7X_VMEM_LIMIT_BYTES = 56 * 1024 * 1024
LANES = 128
CONV_HALO = 8

ROW_TILE = 512
CONV_ROWS = 64
DN_TILE = 256
DN_BATCH = 4
KV_GROUP = 2
Q_BLOCKS = 2
BF16_SUBLANES = 16
MASK_VALUE = -1e30
LOG2_E = 1.4426950408889634

NT_DIMS = (((1,), (1,)), ((), ()))


def _silu(v):
    return v * jax.nn.sigmoid(v)


def _dot(a, b):
    return jnp.dot(a, b, preferred_element_type=F32)


def _dot_nt(a, b):
    return lax.dot_general(a, b, NT_DIMS, preferred_element_type=F32)


def _split_bf16(v):
    hi = v.astype(BF16)
    lo = (v - hi.astype(F32)).astype(BF16)
    return hi, lo


def _mod_kernel(c_ref, w_ref, b_ref, o_ref):
    sc = _silu(c_ref[...])
    o_ref[0] = jnp.dot(sc, w_ref[0], preferred_element_type=F32,
                       precision=lax.Precision.HIGHEST) + b_ref[0]


def _modulation(c, w_ada, b_ada):
    depth, d, d3 = w_ada.shape
    b = c.shape[0]
    tn = 512
    return pl.pallas_call(
        _mod_kernel,
        grid=(depth, d3 // tn),
        in_specs=[
            pl.BlockSpec((b, d), lambda l, j: (0, 0)),
            pl.BlockSpec((1, d, tn), lambda l, j: (l, 0, j)),
            pl.BlockSpec((1, 1, tn), lambda l, j: (l, 0, j)),
        ],
        out_specs=pl.BlockSpec((1, b, tn), lambda l, j: (l, 0, j)),
        out_shape=jax.ShapeDtypeStruct((depth, b, d3), F32),
        name="adaln_modulation",
    )(c, w_ada, b_ada.reshape(depth, 1, d3))


def _inproj_kernel(x_ref, mod_ref, gpre_ref, wtok_ref, wvt_ref, wbat_ref, convw_ref,
                   qkv_dn_ref, z_dn_ref, qk_mb_ref, z_mb_ref, merge_ref, ba_ref, vt_ref, bat_ref,
                   stage_ref, hist_ref, *, d_model):
    ti = pl.program_id(1)
    tm = x_ref.shape[1]
    dk = DN_HEAD_DIM

    @pl.when(ti == 0)
    def _():
        hist_ref[...] = jnp.zeros(hist_ref.shape, F32)

    x = x_ref[0]
    shift = mod_ref[0, :, 0:d_model]
    scale = mod_ref[0, :, d_model:2 * d_model]
    xn = x * lax.rsqrt(jnp.mean(x * x, axis=-1, keepdims=True) + NORM_EPS)
    h = (xn * gpre_ref[...]) * (1.0 + scale) + shift
    hb = h.astype(BF16)

    def conv_part(part):
        c0 = part * DN_WIDTH
        for hd in range(DN_HEADS):
            l0 = hd * dk
            for t0 in range(0, tm, CONV_ROWS):
                xs = stage_ref[part, t0:t0 + CONV_HALO + CONV_ROWS, l0:l0 + dk]
                acc = convw_ref[0:1, c0 + l0:c0 + l0 + dk] * xs
                for k in range(1, DN_CONV):
                    acc = pltpu.roll(acc, 1, 0) + convw_ref[k:k + 1, c0 + l0:c0 + l0 + dk] * xs
                y = _silu(acc[CONV_HALO:CONV_HALO + CONV_ROWS])
                if part < 2:
                    inv = lax.rsqrt(jnp.sum(y * y, axis=-1, keepdims=True) + NORM_EPS)
                    y = y * (inv * (dk ** -0.5) if part == 0 else inv)
                qkv_dn_ref[0, t0:t0 + CONV_ROWS, c0 + l0:c0 + l0 + dk] = y.astype(qkv_dn_ref.dtype)

    plain = []
    col = 3 * DN_WIDTH
    for ref in (z_dn_ref, qk_mb_ref, z_mb_ref, merge_ref, ba_ref):
        width = ref.shape[-1]
        step = min(width, 512)
        plain += [(ref, c0, step, col + c0) for c0 in range(0, width, step)]
        col += width
    per_part = -(-len(plain) // 3)
    for part in range(3):
        c0 = part * DN_WIDTH
        stage_ref[part, 0:CONV_HALO, :] = hist_ref[part]
        stage_ref[part, CONV_HALO:CONV_HALO + tm, :] = _dot(hb, wtok_ref[:, c0:c0 + DN_WIDTH])
        hist_ref[part] = stage_ref[part, tm:tm + CONV_HALO, :]
        for ref, r0, step, wcol in plain[part * per_part:(part + 1) * per_part]:
            ref[0, :, r0:r0 + step] = _dot(hb, wtok_ref[:, wcol:wcol + step]).astype(ref.dtype)
        conv_part(part)

    vt = _dot_nt(wvt_ref[...], hb).astype(BF16)
    for j in range(vt_ref.shape[1]):
        vt_ref[0, j] = vt[:, j * MB_BLOCK:(j + 1) * MB_BLOCK]
    bat_ref[0] = _dot_nt(wbat_ref[...], hb)


def _in_projection(x, mod, g_pre, w_tok, w_vt, w_bat, conv_w):
    b, s, d = x.shape
    tm = ROW_TILE
    n_tok = w_tok.shape[1]
    kern = functools.partial(_inproj_kernel, d_model=d)
    const = lambda bi, ti: (0, 0)
    row = lambda bi, ti: (bi, ti, 0)
    widths = [3 * DN_WIDTH, DN_WIDTH, 2 * MB_WIDTH, MB_WIDTH, 2 * d, LANES]
    dtypes = [BF16, BF16, BF16, BF16, BF16, F32]
    out_shape = [jax.ShapeDtypeStruct((b, s, w), t) for w, t in zip(widths, dtypes)]
    out_specs = [pl.BlockSpec((1, tm, w), row) for w in widths]
    out_shape += [
        jax.ShapeDtypeStruct((b, s // MB_BLOCK, MB_WIDTH, MB_BLOCK), BF16),
        jax.ShapeDtypeStruct((b, 2 * DN_HEADS, s), F32),
    ]
    out_specs += [
        pl.BlockSpec((1, tm // MB_BLOCK, MB_WIDTH, MB_BLOCK), lambda bi, ti: (bi, ti, 0, 0)),
        pl.BlockSpec((1, 2 * DN_HEADS, tm), lambda bi, ti: (bi, 0, ti)),
    ]
    return pl.pallas_call(
        kern,
        grid=(b, s // tm),
        in_specs=[
            pl.BlockSpec((1, tm, d), row),
            pl.BlockSpec((1, 1, 3 * d), lambda bi, ti: (bi, 0, 0)),
            pl.BlockSpec((1, d), const),
            pl.BlockSpec((d, n_tok), const, pipeline_mode=pl.Buffered(1)),
            pl.BlockSpec((MB_WIDTH, d), const, pipeline_mode=pl.Buffered(1)),
            pl.BlockSpec((2 * DN_HEADS, d), const, pipeline_mode=pl.Buffered(1)),
            pl.BlockSpec((DN_CONV, 3 * DN_WIDTH), const),
        ],
        out_specs=out_specs,
        out_shape=out_shape,
        scratch_shapes=[
            pltpu.VMEM((3, CONV_HALO + tm, DN_WIDTH), F32),
            pltpu.VMEM((3, CONV_HALO, DN_WIDTH), F32),
        ],
        compiler_params=pltpu.CompilerParams(
            dimension_semantics=("parallel", "arbitrary"),
            vmem_limit_bytes=V7X_VMEM_LIMIT_BYTES),
        name="in_projection",
    )(x, mod, g_pre, w_tok, w_vt, w_bat, conv_w)


def _dn_kernel(qkv_ref, ba_ref, bat_ref, z_ref, pcol_ref, prow_ref, gnorm_ref, o_ref, state_ref):
    t = pl.program_id(1)
    nbat, tt = qkv_ref.shape[0], qkv_ref.shape[1]
    n_chunks = tt // DN_CHUNK
    dk = DN_HEAD_DIM
    cs = DN_CHUNK

    @pl.when(t == 0)
    def _():
        state_ref[...] = jnp.zeros(state_ref.shape, F32)

    ri = lax.broadcasted_iota(jnp.int32, (tt, tt), 0)
    ci = lax.broadcasted_iota(jnp.int32, (tt, tt), 1)
    same_chunk = (ri // DN_CHUNK) == (ci // DN_CHUNK)
    lower = (same_chunk & (ri >= ci)).astype(BF16)
    upper = (same_chunk & (ri <= ci)).astype(BF16)

    beta_all, gc_row, eg, eg_to_last, eg_last = [], [], [], [], []
    for bi in range(nbat):
        ba = ba_ref[bi]
        beta_all.append(jax.nn.sigmoid(ba))
        g_col = -jnp.exp(pcol_ref[0:1, :]) * jax.nn.softplus(ba + pcol_ref[1:2, :])
        g_row = -jnp.exp(prow_ref[:, 0:1]) * jax.nn.softplus(bat_ref[bi] + prow_ref[:, 1:2])
        gch, gcl = _split_bf16(g_col)
        gc_col = _dot(lower, gch) + _dot(lower, gcl)
        grh, grl = _split_bf16(g_row)
        gc_row.append(_dot(grh, upper) + _dot(grl, upper))
        eg.append(jnp.exp(gc_col))
        g_last = [gc_col[(c + 1) * DN_CHUNK - 1:(c + 1) * DN_CHUNK, :] for c in range(n_chunks)]
        eg_to_last.append(jnp.concatenate(
            [jnp.exp(g_last[c] - gc_col[c * DN_CHUNK:(c + 1) * DN_CHUNK, :]) for c in range(n_chunks)], axis=0))
        eg_last.append([jnp.exp(g_last[c]) for c in range(n_chunks)])

    col_chunk = lax.broadcasted_iota(jnp.int32, (dk, tt), 1) // cs

    wr = lax.broadcasted_iota(jnp.int32, (cs, tt), 0)
    wl = lax.broadcasted_iota(jnp.int32, (cs, tt), 1)
    wj = wl % cs
    wc = wl // cs
    causal_w = wr >= wj
    strict_w = wr > wj
    same_chunk_b = same_chunk.astype(BF16)

    def fold(x):
        acc = None
        for c in range(n_chunks):
            part = jnp.where(wc == c, x[c * cs:(c + 1) * cs, :], 0.0)
            acc = part if acc is None else acc + part
        return acc

    def block_diag(xw):
        return jnp.where(same_chunk, jnp.concatenate([xw] * n_chunks, axis=0), jnp.zeros((), xw.dtype))

    def blocks_w(n):
        return (wr // n) == (wj // n)

    heads = range(nbat * DN_HEADS)
    bat = [u // DN_HEADS for u in heads]
    hd = [u % DN_HEADS for u in heads]
    gl = [DN_HEADS + hd[h] for h in heads]
    q_b = [qkv_ref[bat[h], :, hd[h] * dk:(hd[h] + 1) * dk] for h in heads]
    kn_b = [qkv_ref[bat[h], :, DN_WIDTH + hd[h] * dk:DN_WIDTH + (hd[h] + 1) * dk] for h in heads]
    qn = [q_b[h].astype(F32) for h in heads]
    kn = [kn_b[h].astype(F32) for h in heads]
    beta = [beta_all[bat[h]][:, hd[h]:hd[h] + 1] for h in heads]
    kb = [kn[h] * beta[h] for h in heads]
    vb = [qkv_ref[bat[h], :, 2 * DN_WIDTH + hd[h] * dk:2 * DN_WIDTH + (hd[h] + 1) * dk].astype(F32) * beta[h]
          for h in heads]
    scores = [_dot_nt(jnp.concatenate([kb[h].astype(BF16), q_b[h]], axis=0), kn_b[h]) for h in heads]
    g_rows = [gc_row[bat[h]][gl[h]:gl[h] + 1, :] for h in heads]
    gcol_w = []
    for h in heads:
        rh, rl = _split_bf16(jnp.where(wj == wr, g_rows[h], 0.0))
        gcol_w.append(_dot(rh, same_chunk_b) + _dot(rl, same_chunk_b))

    a, qk = [], []
    for h in heads:
        diff = gcol_w[h] - g_rows[h]
        decay = jnp.where(causal_w, jnp.exp(jnp.where(causal_w, diff, 0.0)), 0.0)
        a.append(jnp.where(strict_w, fold(scores[h][0:tt]) * decay, 0.0))
        qk.append(fold(scores[h][tt:2 * tt]) * decay)

    p = [(wr == wj).astype(F32) - jnp.where(blocks_w(2), a[h], 0.0) for h in heads]
    n = 2
    while n < cs:
        lower_left = blocks_w(2 * n) & jnp.logical_not(blocks_w(n))
        p_b = [p[h].astype(BF16) for h in heads]
        mid = [_dot(jnp.where(lower_left, a[h], 0.0).astype(BF16), block_diag(p_b[h])) for h in heads]
        upd = [_dot(p_b[h], block_diag(mid[h].astype(BF16))) for h in heads]
        p = [p[h] - upd[h] for h in heads]
        n *= 2
    p = [block_diag(p[h].astype(BF16)) for h in heads]

    eg_h = [eg[bat[h]][:, gl[h]:gl[h] + 1] for h in heads]
    uw = [_dot(p[h], jnp.concatenate([vb[h].astype(BF16), (kb[h] * eg_h[h]).astype(BF16)], axis=1))
          for h in heads]
    u = [uw[h][:, 0:dk] for h in heads]
    w_b = [uw[h][:, dk:2 * dk].astype(BF16) for h in heads]
    qg_b = [(qn[h] * eg_h[h]).astype(BF16) for h in heads]
    kdt = [(kn[h] * eg_to_last[bat[h]][:, gl[h]:gl[h] + 1]).T for h in heads]

    state = [state_ref[h] for h in heads]
    v_rows = [[] for _ in heads]
    o_rows = [[] for _ in heads]
    for c in range(n_chunks):
        rows = slice(c * DN_CHUNK, (c + 1) * DN_CHUNK)
        r2 = [_dot(jnp.concatenate([w_b[h][rows], qg_b[h][rows]], axis=0), state[h].astype(BF16)) for h in heads]
        lhs3 = []
        for h in heads:
            kdt_c = jnp.where(col_chunk == c, kdt[h], 0.0).astype(BF16)
            qk_c = jnp.where(wc == c, qk[h], 0.0).astype(BF16)
            lhs3.append(jnp.concatenate([qk_c, kdt_c], axis=0))
        r3 = []
        for h in heads:
            v_rows[h].append((u[h][rows] - r2[h][0:DN_CHUNK]).astype(BF16))
            v_blk = jnp.concatenate(
                v_rows[h] + [jnp.zeros(((n_chunks - 1 - c) * DN_CHUNK, dk), BF16)] * (c < n_chunks - 1), axis=0)
            r3.append(_dot(lhs3[h], v_blk))
        for h in heads:
            o_rows[h].append(r2[h][DN_CHUNK:2 * DN_CHUNK] + r3[h][0:DN_CHUNK])
            state[h] = state[h] * eg_last[bat[h]][c][:, gl[h]:gl[h] + 1] + r3[h][DN_CHUNK:DN_CHUNK + dk]

    for h in heads:
        state_ref[h] = state[h]
        o = jnp.concatenate(o_rows[h], axis=0)
        on = o * lax.rsqrt(jnp.mean(o * o, axis=-1, keepdims=True) + NORM_EPS) * gnorm_ref[...]
        zh = z_ref[bat[h], :, hd[h] * dk:(hd[h] + 1) * dk].astype(F32)
        o_ref[bat[h], :, hd[h] * dk:(hd[h] + 1) * dk] = (on * _silu(zh)).astype(o_ref.dtype)


def _deltanet(qkv, ba, bat, z, p_col, p_row, g_norm):
    b, s, w3 = qkv.shape
    tt = DN_TILE
    nbat = DN_BATCH if b % DN_BATCH == 0 else 1
    row = lambda bi, ti: (bi, ti, 0)
    const = lambda bi, ti: (0, 0)
    return pl.pallas_call(
        _dn_kernel,
        grid=(b // nbat, s // tt),
        in_specs=[
            pl.BlockSpec((nbat, tt, w3), row),
            pl.BlockSpec((nbat, tt, LANES), row),
            pl.BlockSpec((nbat, 2 * DN_HEADS, tt), lambda bi, ti: (bi, 0, ti)),
            pl.BlockSpec((nbat, tt, DN_WIDTH), row),
            pl.BlockSpec((2, LANES), const),
            pl.BlockSpec((2 * DN_HEADS, 2), const),
            pl.BlockSpec((1, DN_HEAD_DIM), const),
        ],
        out_specs=pl.BlockSpec((nbat, tt, DN_WIDTH), row),
        out_shape=jax.ShapeDtypeStruct((b, s, DN_WIDTH), BF16),
        scratch_shapes=[
            pltpu.VMEM((nbat * DN_HEADS, DN_HEAD_DIM, DN_HEAD_DIM), F32),
        ],
        compiler_params=pltpu.CompilerParams(
            dimension_semantics=("parallel", "arbitrary"),
            vmem_limit_bytes=V7X_VMEM_LIMIT_BYTES),
        name="deltanet",
    )(qkv, ba, bat, z, p_col, p_row, g_norm)


def _moba_kernel(q_ref, k_ref, vt_ref, z_ref, o_ref, kmean_ref, kaug_ref, qaug_ref, s_ref):
    a = pl.program_id(2)
    i = a * Q_BLOCKS
    nb = vt_ref.shape[1]
    blk = MB_BLOCK
    qw = Q_BLOCKS * blk
    dh = MB_HEAD_DIM
    heads = LANES // dh
    lane = lax.broadcasted_iota(jnp.int32, (1, LANES), 1)
    in_head = [(lane >= h * dh) & (lane < (h + 1) * dh) for h in range(heads)]
    bias_lane0 = [((h + 1) % heads) * dh for h in range(heads)]

    @pl.when(a == 0)
    def _():
        for j in range(nb):
            kj = k_ref[0, j * blk:(j + 1) * blk, :]
            kmean_ref[j:j + 1, :] = jnp.mean(kj.astype(F32), axis=0, keepdims=True)
            for h in range(heads):
                onehot = (lane == bias_lane0[h] + j).astype(BF16)
                kaug_ref[h, j * blk:(j + 1) * blk, :] = jnp.where(in_head[h], kj, onehot)
        for h in range(heads):
            kaug_ref[h, nb * blk:(nb + 1) * blk, :] = jnp.broadcast_to(
                (lane == bias_lane0[h] + nb).astype(BF16), (blk, LANES))
        km = kmean_ref[...]
        pieces = []
        for h in range(heads):
            rest = jnp.where(in_head[h], km, 0.0)
            for _ in range(3):
                part = rest.astype(BF16)
                pieces.append(part)
                rest = rest - part.astype(F32)
        km_parts = jnp.concatenate(pieces, axis=0)
        jidx = lax.broadcasted_iota(jnp.int32, (nb, blk), 0)
        gates = []
        for iq in range(nb):
            g_all = _dot_nt(km_parts, q_ref[0, iq * blk:(iq + 1) * blk, :])
            gates.append([g_all[(3 * h) * nb:(3 * h + 1) * nb] + g_all[(3 * h + 1) * nb:(3 * h + 2) * nb]
                          + g_all[(3 * h + 2) * nb:(3 * h + 3) * nb] for h in range(heads)])
        pad_rows = dh - nb
        for iq in range(nb):
            bias = []
            for h in range(heads):
                gate = jnp.where(jidx < iq, gates[iq][h], -jnp.inf)
                rank = jnp.zeros((nb, blk), jnp.int32)
                for jp in range(iq):
                    other = gate[jp:jp + 1, :]
                    beats = (other > gate) | ((other == gate) & (jp < jidx))
                    rank = rank + beats.astype(jnp.int32)
                keep = (rank < MB_TOPK) & (jidx < iq)
                bias.append(jnp.where(keep, 0.0, MASK_VALUE))
            bias_rows = []
            for h in range(heads):
                other = (h + 1) % heads
                bias_rows += [bias[other], jnp.full((1, blk), MASK_VALUE, F32), jnp.zeros((pad_rows - 1, blk), F32)]
            bias_t = jnp.concatenate(bias_rows, axis=0).T.astype(BF16)
            q_blk = q_ref[0, iq * blk:(iq + 1) * blk, :]
            for h in range(heads):
                qaug_ref[h, iq * blk:(iq + 1) * blk, :] = jnp.where(in_head[h], q_blk, bias_t)

    q_aug = [qaug_ref[h, pl.ds(pl.multiple_of(a * qw, qw), qw), :] for h in range(heads)]
    ones_rows = jnp.ones((BF16_SUBLANES, blk), BF16)

    def v_ext(jb, h):
        return jnp.concatenate([vt_ref[0, jb, h * dh:(h + 1) * dh, :], ones_rows], axis=0)

    def score_dots(t):
        tiles = []
        for h in range(heads):
            for g in range(KV_GROUP):
                j = t * KV_GROUP + g
                kb = jnp.where(j < i, j, nb)
                tiles.append(_dot_nt(kaug_ref[h, pl.ds(pl.multiple_of(kb * blk, blk), blk), :], q_aug[h]))
        return tiles

    def stash(tiles):
        maxima = []
        for h in range(heads):
            cm = None
            for g in range(KV_GROUP):
                st = tiles[h * KV_GROUP + g]
                s_ref[h * KV_GROUP + g] = st
                gm = jnp.max(st, axis=0, keepdims=True)
                cm = gm if cm is None else jnp.maximum(cm, gm)
            maxima.append(cm)
        return maxima

    key_pos = lax.broadcasted_iota(jnp.int32, (blk, qw), 0)
    qry_pos = lax.broadcasted_iota(jnp.int32, (blk, qw), 1)
    q_row = lax.broadcasted_iota(jnp.int32, (qw, 1), 0)
    own_tiles = []
    for h in range(heads):
        plain = jnp.where(in_head[h], q_aug[h], jnp.zeros_like(q_aug[h]))
        for d in range(Q_BLOCKS):
            kd = kaug_ref[h, pl.ds(pl.multiple_of((i + d) * blk, blk), blk), :]
            own_tiles.append(_dot_nt(kd, jnp.where(q_row < (d + 1) * blk, plain, q_aug[h])))
    tiles0 = score_dots(0)
    ms, ps = [], []
    for h in range(heads):
        st = []
        for d in range(Q_BLOCKS):
            rel = qry_pos - d * blk
            visible = (rel >= blk) | ((rel >= 0) & (key_pos <= rel))
            st.append(jnp.where(visible, own_tiles[h * Q_BLOCKS + d], MASK_VALUE))
        m = jnp.max(st[0], axis=0, keepdims=True)
        for d in range(1, Q_BLOCKS):
            m = jnp.maximum(m, jnp.max(st[d], axis=0, keepdims=True))
        ms.append(m)
        ps.append([jnp.exp2(st[d] - m).astype(BF16) for d in range(Q_BLOCKS)])
    cms = stash(tiles0)
    carry = []
    for h in range(heads):
        acc = None
        for d in range(Q_BLOCKS):
            term = _dot(v_ext(i + d, h), ps[h][d])
            acc = term if acc is None else acc + term
        carry += [ms[h], acc, cms[h]]

    def body(t, carry):
        tiles_next = score_dots(t + 1)
        new_m, alphas, ps = [], [], []
        for h in range(heads):
            m, _, cm = carry[3 * h:3 * h + 3]
            m_new = jnp.maximum(m, cm)
            new_m.append(m_new)
            alphas.append(jnp.exp2(m - m_new))
            ps.append([jnp.exp2(s_ref[h * KV_GROUP + g] - m_new).astype(BF16) for g in range(KV_GROUP)])
        cm_next = stash(tiles_next)
        out = []
        for h in range(heads):
            pv = None
            for g in range(KV_GROUP):
                vb = jnp.minimum(t * KV_GROUP + g, nb - 1)
                term = _dot(v_ext(vb, h), ps[h][g])
                pv = term if pv is None else pv + term
            out += [new_m[h], alphas[h] * carry[3 * h + 1] + pv, cm_next[h]]
        return tuple(out)

    carry = lax.fori_loop(0, (i + KV_GROUP - 1) // KV_GROUP, body, tuple(carry))
    ot = jnp.concatenate([carry[3 * h + 1][0:dh] / carry[3 * h + 1][dh:dh + 1] for h in range(heads)], axis=0)
    o = ot.T
    o_ref[0] = (o * _silu(z_ref[0].astype(F32))).astype(o_ref.dtype)


def _moba(qk, vt, z):
    b, s, _ = qk.shape
    nb = s // MB_BLOCK
    pairs = MB_WIDTH // LANES
    heads = LANES // MB_HEAD_DIM
    assert nb + 1 <= MB_HEAD_DIM
    assert nb % Q_BLOCKS == 0
    q_rows = Q_BLOCKS * MB_BLOCK
    return pl.pallas_call(
        _moba_kernel,
        grid=(b, pairs, nb // Q_BLOCKS),
        in_specs=[
            pl.BlockSpec((1, s, LANES), lambda bi, hp, i: (bi, 0, hp)),
            pl.BlockSpec((1, s, LANES), lambda bi, hp, i: (bi, 0, pairs + hp)),
            pl.BlockSpec((1, nb, LANES, MB_BLOCK), lambda bi, hp, i: (bi, 0, hp, 0)),
            pl.BlockSpec((1, q_rows, LANES), lambda bi, hp, i: (bi, i, hp)),
        ],
        out_specs=pl.BlockSpec((1, q_rows, LANES), lambda bi, hp, i: (bi, i, hp)),
        out_shape=jax.ShapeDtypeStruct((b, s, MB_WIDTH), BF16),
        scratch_shapes=[
            pltpu.VMEM((nb, LANES), F32),
            pltpu.VMEM((heads, s + MB_BLOCK, LANES), BF16),
            pltpu.VMEM((heads, s, LANES), BF16),
            pltpu.VMEM((heads * KV_GROUP, MB_BLOCK, q_rows), F32),
        ],
        compiler_params=pltpu.CompilerParams(
            dimension_semantics=("parallel", "parallel", "arbitrary"),
            vmem_limit_bytes=V7X_VMEM_LIMIT_BYTES),
        name="moba_attention",
    )(qk, qk, vt, z)


def _outproj_kernel(x_ref, odn_ref, omb_ref, merge_ref, mod_ref, gpost_ref,
                    wdn_ref, wmb_ref, wout_ref, o_ref, *, d_model):
    y_dn = _dot(odn_ref[0], wdn_ref[...])
    y_mb = _dot(omb_ref[0], wmb_ref[...])
    gates = jax.nn.sigmoid(merge_ref[0].astype(F32))
    mixed_in = gates[:, 0:d_model] * y_dn + gates[:, d_model:2 * d_model] * y_mb
    mixed = _dot(mixed_in.astype(BF16), wout_ref[...])
    y = mixed * lax.rsqrt(jnp.mean(mixed * mixed, axis=-1, keepdims=True) + NORM_EPS) * gpost_ref[...]
    gate = mod_ref[0, :, 2 * d_model:3 * d_model]
    o_ref[0] = x_ref[0] + gate * y


def _out_projection(x, o_dn, o_mb, merge, mod, g_post, w_dn, w_mb, w_out):
    b, s, d = x.shape
    tm = ROW_TILE
    row = lambda bi, ti: (bi, ti, 0)
    const = lambda bi, ti: (0, 0)
    kern = functools.partial(_outproj_kernel, d_model=d)
    return pl.pallas_call(
        kern,
        grid=(b, s // tm),
        in_specs=[
            pl.BlockSpec((1, tm, d), row),
            pl.BlockSpec((1, tm, DN_WIDTH), row),
            pl.BlockSpec((1, tm, MB_WIDTH), row),
            pl.BlockSpec((1, tm, 2 * d), row),
            pl.BlockSpec((1, 1, 3 * d), lambda bi, ti: (bi, 0, 0)),
            pl.BlockSpec((1, d), const),
            pl.BlockSpec((DN_WIDTH, d), const),
            pl.BlockSpec((MB_WIDTH, d), const),
            pl.BlockSpec((d, d), const),
        ],
        out_specs=pl.BlockSpec((1, tm, d), row),
        out_shape=jax.ShapeDtypeStruct((b, s, d), F32),
        compiler_params=pltpu.CompilerParams(
            dimension_semantics=("parallel", "parallel"),
            vmem_limit_bytes=V7X_VMEM_LIMIT_BYTES),
        name="out_projection",
    )(x, o_dn, o_mb, merge, mod, g_post, w_dn, w_mb, w_out)


def _split_w_in(w_in, d_model):
    sizes = (3 * DN_WIDTH, DN_WIDTH, DN_HEADS, DN_HEADS, 3 * MB_WIDTH, MB_WIDTH, 2 * d_model)
    cuts = [0]
    for n in sizes:
        cuts.append(cuts[-1] + n)
    w_qkv_dn, w_z_dn, w_beta, w_a, w_qkv_mb, w_z_mb, w_merge = (
        w_in[:, cuts[i]:cuts[i + 1]] for i in range(len(sizes)))
    w_ba = jnp.concatenate([w_beta, w_a], axis=1)
    w_ba_pad = jnp.pad(w_ba, ((0, 0), (0, LANES - 2 * DN_HEADS)))
    w_q_mb = w_qkv_mb[:, 0:MB_WIDTH] * (MB_HEAD_DIM ** -0.5 * LOG2_E)
    w_k_mb = w_qkv_mb[:, MB_WIDTH:2 * MB_WIDTH]
    w_tok = jnp.concatenate(
        [w_qkv_dn, w_z_dn, w_q_mb, w_k_mb, w_z_mb, w_merge, w_ba_pad], axis=1).astype(BF16)
    w_vt = w_qkv_mb[:, 2 * MB_WIDTH:3 * MB_WIDTH].T.astype(BF16)
    w_bat = w_ba.T.astype(BF16)
    return w_tok, w_vt, w_bat


def _layer(x, mod, g_pre, g_post, w_in, conv_w, a_log, dt_bias, dn_norm_g, w_proj_dn, w_proj_mb, w_out):
    b, s, d = x.shape
    w_tok, w_vt, w_bat = _split_w_in(w_in, d)
    qkv_dn, z_dn, qk_mb, z_mb, merge, ba, vt, bat = _in_projection(
        x, mod, g_pre.reshape(1, d), w_tok, w_vt, w_bat, conv_w.astype(F32))

    zeros_h = jnp.zeros((DN_HEADS,), F32)
    a_vec = jnp.concatenate([zeros_h, a_log.astype(F32)])
    dt_vec = jnp.concatenate([zeros_h, dt_bias.astype(F32)])
    p_col = jnp.pad(jnp.stack([a_vec, dt_vec]), ((0, 0), (0, LANES - 2 * DN_HEADS)))
    p_row = jnp.stack([a_vec, dt_vec], axis=1)
    o_dn = _deltanet(qkv_dn, ba, bat, z_dn, p_col, p_row, dn_norm_g.reshape(1, DN_HEAD_DIM).astype(F32))
    o_mb = _moba(qk_mb, vt, z_mb)
    return _out_projection(x, o_dn, o_mb, merge, mod, g_post.reshape(1, d),
                           w_proj_dn.astype(BF16), w_proj_mb.astype(BF16), w_out.astype(BF16))


def kernel(x, c, w_ada, b_ada, g_pre, g_post, w_in, conv_w, a_log, dt_bias, dn_norm_g, w_proj_dn, w_proj_mb, w_out):
    depth = w_ada.shape[0]
    b, s, d = x.shape
    assert s % ROW_TILE == 0 and s % MB_BLOCK == 0 and s % DN_TILE == 0
    mod_all = _modulation(c, w_ada, b_ada)
    for l in range(depth):
        mod = mod_all[l].reshape(b, 1, 3 * d)
        x = _layer(x, mod, g_pre[l], g_post[l], w_in[l], conv_w[l], a_log[l], dt_bias[l],
                   dn_norm_g[l], w_proj_dn[l], w_proj_mb[l], w_out[l])
    return x
```

```python
import functools

import jax
import jax.numpy as jnp
from jax import lax
from jax.experimental import pallas as pl
from jax.experimental.pallas import tpu as pltpu

F32 = jnp.float32
BF16 = jnp.bfloat16

NORM_EPS = 1e-6
DN_HEADS = 4
DN_HEAD_DIM = 128
DN_WIDTH = DN_HEADS * DN_HEAD_DIM
DN_CONV = 4
DN_CHUNK = 64
MB_HEADS = 8
MB_HEAD_DIM = 64
MB_WIDTH = MB_HEADS * MB_HEAD_DIM
MB_BLOCK = 256
MB_TOPK = 3

V7X_VMEM_LIMIT_BYTES = 56 * 1024 * 1024
LANES = 128
CONV_HALO = 8

ROW_TILE = 512
CONV_ROWS = 64
DN_TILE = 256
DN_BATCH = 4
KV_GROUP = 2
Q_BLOCKS = 2
BF16_SUBLANES = 16
MASK_VALUE = -1e30
LOG2_E = 1.4426950408889634

NT_DIMS = (((1,), (1,)), ((), ()))


def _silu(v):
    return v * jax.nn.sigmoid(v)


def _dot(a, b):
    return jnp.dot(a, b, preferred_element_type=F32)


def _dot_nt(a, b):
    return lax.dot_general(a, b, NT_DIMS, preferred_element_type=F32)


def _split_bf16(v):
    hi = v.astype(BF16)
    lo = (v - hi.astype(F32)).astype(BF16)
    return hi, lo


def _mod_kernel(c_ref, w_ref, b_ref, o_ref):
    sc = _silu(c_ref[...])
    o_ref[0] = jnp.dot(sc, w_ref[0], preferred_element_type=F32,
                       precision=lax.Precision.HIGHEST) + b_ref[0]


def _modulation(c, w_ada, b_ada):
    depth, d, d3 = w_ada.shape
    b = c.shape[0]
    tn = 512
    return pl.pallas_call(
        _mod_kernel,
        grid=(depth, d3 // tn),
        in_specs=[
            pl.BlockSpec((b, d), lambda l, j: (0, 0)),
            pl.BlockSpec((1, d, tn), lambda l, j: (l, 0, j)),
            pl.BlockSpec((1, 1, tn), lambda l, j: (l, 0, j)),
        ],
        out_specs=pl.BlockSpec((1, b, tn), lambda l, j: (l, 0, j)),
        out_shape=jax.ShapeDtypeStruct((depth, b, d3), F32),
        name="adaln_modulation",
    )(c, w_ada, b_ada.reshape(depth, 1, d3))


def _inproj_kernel(x_ref, mod_ref, gpre_ref, wtok_ref, wvt_ref, wbat_ref, convw_ref,
                   qkv_dn_ref, z_dn_ref, qk_mb_ref, z_mb_ref, merge_ref, ba_ref, vt_ref, bat_ref,
                   stage_ref, hist_ref, *, d_model):
    ti = pl.program_id(1)
    tm = x_ref.shape[1]
    dk = DN_HEAD_DIM

    @pl.when(ti == 0)
    def _():
        hist_ref[...] = jnp.zeros(hist_ref.shape, F32)

    x = x_ref[0]
    shift = mod_ref[0, :, 0:d_model]
    scale = mod_ref[0, :, d_model:2 * d_model]
    xn = x * lax.rsqrt(jnp.mean(x * x, axis=-1, keepdims=True) + NORM_EPS)
    h = (xn * gpre_ref[...]) * (1.0 + scale) + shift
    hb = h.astype(BF16)

    def conv_part(part):
        c0 = part * DN_WIDTH
        for hd in range(DN_HEADS):
            l0 = hd * dk
            for t0 in range(0, tm, CONV_ROWS):
                xs = stage_ref[part, t0:t0 + CONV_HALO + CONV_ROWS, l0:l0 + dk]
                acc = convw_ref[0:1, c0 + l0:c0 + l0 + dk] * xs
                for k in range(1, DN_CONV):
                    acc = pltpu.roll(acc, 1, 0) + convw_ref[k:k + 1, c0 + l0:c0 + l0 + dk] * xs
                y = _silu(acc[CONV_HALO:CONV_HALO + CONV_ROWS])
                if part < 2:
                    inv = lax.rsqrt(jnp.sum(y * y, axis=-1, keepdims=True) + NORM_EPS)
                    y = y * (inv * (dk ** -0.5) if part == 0 else inv)
                qkv_dn_ref[0, t0:t0 + CONV_ROWS, c0 + l0:c0 + l0 + dk] = y.astype(qkv_dn_ref.dtype)

    plain = []
    col = 3 * DN_WIDTH
    for ref in (z_dn_ref, qk_mb_ref, z_mb_ref, merge_ref, ba_ref):
        width = ref.shape[-1]
        step = min(width, 512)
        plain += [(ref, c0, step, col + c0) for c0 in range(0, width, step)]
        col += width
    per_part = -(-len(plain) // 3)
    for part in range(3):
        c0 = part * DN_WIDTH
        stage_ref[part, 0:CONV_HALO, :] = hist_ref[part]
        stage_ref[part, CONV_HALO:CONV_HALO + tm, :] = _dot(hb, wtok_ref[0, :, c0:c0 + DN_WIDTH])
        hist_ref[part] = stage_ref[part, tm:tm + CONV_HALO, :]
        for ref, r0, step, wcol in plain[part * per_part:(part + 1) * per_part]:
            ref[0, :, r0:r0 + step] = _dot(hb, wtok_ref[0, :, wcol:wcol + step]).astype(ref.dtype)
        conv_part(part)

    vt = _dot_nt(wvt_ref[0], hb).astype(BF16)
    for j in range(vt_ref.shape[1]):
        vt_ref[0, j] = vt[:, j * MB_BLOCK:(j + 1) * MB_BLOCK]
    bat_ref[0] = _dot_nt(wbat_ref[0], hb)


def _in_projection(x, mod, g_pre, w_tok, w_vt, w_bat, conv_w, layer):
    b, s, d = x.shape
    tm = ROW_TILE
    n_tok = w_tok.shape[2]
    this_layer = lambda bi, ti: (layer, 0, 0)
    kern = functools.partial(_inproj_kernel, d_model=d)
    const = lambda bi, ti: (0, 0)
    row = lambda bi, ti: (bi, ti, 0)
    widths = [3 * DN_WIDTH, DN_WIDTH, 2 * MB_WIDTH, MB_WIDTH, 2 * d, LANES]
    dtypes = [BF16, BF16, BF16, BF16, BF16, F32]
    out_shape = [jax.ShapeDtypeStruct((b, s, w), t) for w, t in zip(widths, dtypes)]
    out_specs = [pl.BlockSpec((1, tm, w), row) for w in widths]
    out_shape += [
        jax.ShapeDtypeStruct((b, s // MB_BLOCK, MB_WIDTH, MB_BLOCK), BF16),
        jax.ShapeDtypeStruct((b, 2 * DN_HEADS, s), F32),
    ]
    out_specs += [
        pl.BlockSpec((1, tm // MB_BLOCK, MB_WIDTH, MB_BLOCK), lambda bi, ti: (bi, ti, 0, 0)),
        pl.BlockSpec((1, 2 * DN_HEADS, tm), lambda bi, ti: (bi, 0, ti)),
    ]
    return pl.pallas_call(
        kern,
        grid=(b, s // tm),
        in_specs=[
            pl.BlockSpec((1, tm, d), row),
            pl.BlockSpec((1, 1, 3 * d), lambda bi, ti: (bi, 0, 0)),
            pl.BlockSpec((1, d), const),
            pl.BlockSpec((1, d, n_tok), this_layer, pipeline_mode=pl.Buffered(1)),
            pl.BlockSpec((1, MB_WIDTH, d), this_layer, pipeline_mode=pl.Buffered(1)),
            pl.BlockSpec((1, 2 * DN_HEADS, d), this_layer, pipeline_mode=pl.Buffered(1)),
            pl.BlockSpec((DN_CONV, 3 * DN_WIDTH), const),
        ],
        out_specs=out_specs,
        out_shape=out_shape,
        scratch_shapes=[
            pltpu.VMEM((3, CONV_HALO + tm, DN_WIDTH), F32),
            pltpu.VMEM((3, CONV_HALO, DN_WIDTH), F32),
        ],
        compiler_params=pltpu.CompilerParams(
            dimension_semantics=("parallel", "arbitrary"),
            vmem_limit_bytes=V7X_VMEM_LIMIT_BYTES),
        name="in_projection",
    )(x, mod, g_pre, w_tok, w_vt, w_bat, conv_w)


def _dn_kernel(qkv_ref, ba_ref, bat_ref, z_ref, pcol_ref, prow_ref, gnorm_ref, o_ref, state_ref):
    t = pl.program_id(1)
    nbat, tt = qkv_ref.shape[0], qkv_ref.shape[1]
    n_chunks = tt // DN_CHUNK
    dk = DN_HEAD_DIM
    cs = DN_CHUNK

    @pl.when(t == 0)
    def _():
        state_ref[...] = jnp.zeros(state_ref.shape, F32)

    ri = lax.broadcasted_iota(jnp.int32, (tt, tt), 0)
    ci = lax.broadcasted_iota(jnp.int32, (tt, tt), 1)
    same_chunk = (ri // DN_CHUNK) == (ci // DN_CHUNK)
    lower = (same_chunk & (ri >= ci)).astype(BF16)
    upper = (same_chunk & (ri <= ci)).astype(BF16)

    beta_all, gc_row, eg, eg_to_last, eg_last = [], [], [], [], []
    for bi in range(nbat):
        ba = ba_ref[bi]
        beta_all.append(jax.nn.sigmoid(ba))
        g_col = -jnp.exp(pcol_ref[0:1, :]) * jax.nn.softplus(ba + pcol_ref[1:2, :])
        g_row = -jnp.exp(prow_ref[:, 0:1]) * jax.nn.softplus(bat_ref[bi] + prow_ref[:, 1:2])
        gch, gcl = _split_bf16(g_col)
        gc_col = _dot(lower, gch) + _dot(lower, gcl)
        grh, grl = _split_bf16(g_row)
        gc_row.append(_dot(grh, upper) + _dot(grl, upper))
        eg.append(jnp.exp(gc_col))
        g_last = [gc_col[(c + 1) * DN_CHUNK - 1:(c + 1) * DN_CHUNK, :] for c in range(n_chunks)]
        eg_to_last.append(jnp.concatenate(
            [jnp.exp(g_last[c] - gc_col[c * DN_CHUNK:(c + 1) * DN_CHUNK, :]) for c in range(n_chunks)], axis=0))
        eg_last.append([jnp.exp(g_last[c]) for c in range(n_chunks)])

    col_chunk = lax.broadcasted_iota(jnp.int32, (dk, tt), 1) // cs

    wr = lax.broadcasted_iota(jnp.int32, (cs, tt), 0)
    wl = lax.broadcasted_iota(jnp.int32, (cs, tt), 1)
    wj = wl % cs
    wc = wl // cs
    causal_w = wr >= wj
    strict_w = wr > wj
    same_chunk_b = same_chunk.astype(BF16)

    def fold(x):
        acc = None
        for c in range(n_chunks):
            part = jnp.where(wc == c, x[c * cs:(c + 1) * cs, :], 0.0)
            acc = part if acc is None else acc + part
        return acc

    def block_diag(xw):
        return jnp.where(same_chunk, jnp.concatenate([xw] * n_chunks, axis=0), jnp.zeros((), xw.dtype))

    def blocks_w(n):
        return (wr // n) == (wj // n)

    heads = range(nbat * DN_HEADS)
    bat = [u // DN_HEADS for u in heads]
    hd = [u % DN_HEADS for u in heads]
    gl = [DN_HEADS + hd[h] for h in heads]
    q_b = [qkv_ref[bat[h], :, hd[h] * dk:(hd[h] + 1) * dk] for h in heads]
    kn_b = [qkv_ref[bat[h], :, DN_WIDTH + hd[h] * dk:DN_WIDTH + (hd[h] + 1) * dk] for h in heads]
    qn = [q_b[h].astype(F32) for h in heads]
    kn = [kn_b[h].astype(F32) for h in heads]
    beta = [beta_all[bat[h]][:, hd[h]:hd[h] + 1] for h in heads]
    kb = [kn[h] * beta[h] for h in heads]
    vb = [qkv_ref[bat[h], :, 2 * DN_WIDTH + hd[h] * dk:2 * DN_WIDTH + (hd[h] + 1) * dk].astype(F32) * beta[h]
          for h in heads]
    scores = [_dot_nt(jnp.concatenate([kb[h].astype(BF16), q_b[h]], axis=0), kn_b[h]) for h in heads]
    g_rows = [gc_row[bat[h]][gl[h]:gl[h] + 1, :] for h in heads]
    gcol_w = []
    for h in heads:
        rh, rl = _split_bf16(jnp.where(wj == wr, g_rows[h], 0.0))
        gcol_w.append(_dot(rh, same_chunk_b) + _dot(rl, same_chunk_b))

    a, qk = [], []
    for h in heads:
        diff = gcol_w[h] - g_rows[h]
        decay = jnp.where(causal_w, jnp.exp(jnp.where(causal_w, diff, 0.0)), 0.0)
        a.append(jnp.where(strict_w, fold(scores[h][0:tt]) * decay, 0.0))
        qk.append(fold(scores[h][tt:2 * tt]) * decay)

    p = [(wr == wj).astype(F32) - jnp.where(blocks_w(2), a[h], 0.0) for h in heads]
    n = 2
    while n < cs:
        lower_left = blocks_w(2 * n) & jnp.logical_not(blocks_w(n))
        p_b = [p[h].astype(BF16) for h in heads]
        mid = [_dot(jnp.where(lower_left, a[h], 0.0).astype(BF16), block_diag(p_b[h])) for h in heads]
        upd = [_dot(p_b[h], block_diag(mid[h].astype(BF16))) for h in heads]
        p = [p[h] - upd[h] for h in heads]
        n *= 2
    p = [block_diag(p[h].astype(BF16)) for h in heads]

    eg_h = [eg[bat[h]][:, gl[h]:gl[h] + 1] for h in heads]
    uw = [_dot(p[h], jnp.concatenate([vb[h].astype(BF16), (kb[h] * eg_h[h]).astype(BF16)], axis=1))
          for h in heads]
    u = [uw[h][:, 0:dk] for h in heads]
    w_b = [uw[h][:, dk:2 * dk].astype(BF16) for h in heads]
    qg_b = [(qn[h] * eg_h[h]).astype(BF16) for h in heads]
    kdt = [(kn[h] * eg_to_last[bat[h]][:, gl[h]:gl[h] + 1]).T for h in heads]

    state = [state_ref[h] for h in heads]
    v_rows = [[] for _ in heads]
    o_rows = [[] for _ in heads]
    for c in range(n_chunks):
        rows = slice(c * DN_CHUNK, (c + 1) * DN_CHUNK)
        r2 = [_dot(jnp.concatenate([w_b[h][rows], qg_b[h][rows]], axis=0), state[h].astype(BF16)) for h in heads]
        lhs3 = []
        for h in heads:
            kdt_c = jnp.where(col_chunk == c, kdt[h], 0.0).astype(BF16)
            qk_c = jnp.where(wc == c, qk[h], 0.0).astype(BF16)
            lhs3.append(jnp.concatenate([qk_c, kdt_c], axis=0))
        r3 = []
        for h in heads:
            v_rows[h].append((u[h][rows] - r2[h][0:DN_CHUNK]).astype(BF16))
            v_blk = jnp.concatenate(
                v_rows[h] + [jnp.zeros(((n_chunks - 1 - c) * DN_CHUNK, dk), BF16)] * (c < n_chunks - 1), axis=0)
            r3.append(_dot(lhs3[h], v_blk))
        for h in heads:
            o_rows[h].append(r2[h][DN_CHUNK:2 * DN_CHUNK] + r3[h][0:DN_CHUNK])
            state[h] = state[h] * eg_last[bat[h]][c][:, gl[h]:gl[h] + 1] + r3[h][DN_CHUNK:DN_CHUNK + dk]

    for h in heads:
        state_ref[h] = state[h]
        o = jnp.concatenate(o_rows[h], axis=0)
        on = o * lax.rsqrt(jnp.mean(o * o, axis=-1, keepdims=True) + NORM_EPS) * gnorm_ref[...]
        zh = z_ref[bat[h], :, hd[h] * dk:(hd[h] + 1) * dk].astype(F32)
        o_ref[bat[h], :, hd[h] * dk:(hd[h] + 1) * dk] = (on * _silu(zh)).astype(o_ref.dtype)


def _deltanet(qkv, ba, bat, z, p_col, p_row, g_norm):
    b, s, w3 = qkv.shape
    tt = DN_TILE
    nbat = DN_BATCH if b % DN_BATCH == 0 else 1
    row = lambda bi, ti: (bi, ti, 0)
    const = lambda bi, ti: (0, 0)
    return pl.pallas_call(
        _dn_kernel,
        grid=(b // nbat, s // tt),
        in_specs=[
            pl.BlockSpec((nbat, tt, w3), row),
            pl.BlockSpec((nbat, tt, LANES), row),
            pl.BlockSpec((nbat, 2 * DN_HEADS, tt), lambda bi, ti: (bi, 0, ti)),
            pl.BlockSpec((nbat, tt, DN_WIDTH), row),
            pl.BlockSpec((2, LANES), const),
            pl.BlockSpec((2 * DN_HEADS, 2), const),
            pl.BlockSpec((1, DN_HEAD_DIM), const),
        ],
        out_specs=pl.BlockSpec((nbat, tt, DN_WIDTH), row),
        out_shape=jax.ShapeDtypeStruct((b, s, DN_WIDTH), BF16),
        scratch_shapes=[
            pltpu.VMEM((nbat * DN_HEADS, DN_HEAD_DIM, DN_HEAD_DIM), F32),
        ],
        compiler_params=pltpu.CompilerParams(
            dimension_semantics=("parallel", "arbitrary"),
            vmem_limit_bytes=V7X_VMEM_LIMIT_BYTES),
        name="deltanet",
    )(qkv, ba, bat, z, p_col, p_row, g_norm)


def _moba_kernel(q_ref, k_ref, vt_ref, z_ref, o_ref, kmean_ref, kaug_ref, qaug_ref, s_ref):
    a = pl.program_id(2)
    i = a * Q_BLOCKS
    nb = vt_ref.shape[1]
    blk = MB_BLOCK
    qw = Q_BLOCKS * blk
    dh = MB_HEAD_DIM
    heads = LANES // dh
    lane = lax.broadcasted_iota(jnp.int32, (1, LANES), 1)
    in_head = [(lane >= h * dh) & (lane < (h + 1) * dh) for h in range(heads)]
    bias_lane0 = [((h + 1) % heads) * dh for h in range(heads)]

    @pl.when(a == 0)
    def _():
        for j in range(nb):
            kj = k_ref[0, j * blk:(j + 1) * blk, :]
            kmean_ref[j:j + 1, :] = jnp.mean(kj.astype(F32), axis=0, keepdims=True)
            for h in range(heads):
                onehot = (lane == bias_lane0[h] + j).astype(BF16)
                kaug_ref[h, j * blk:(j + 1) * blk, :] = jnp.where(in_head[h], kj, onehot)
        for h in range(heads):
            kaug_ref[h, nb * blk:(nb + 1) * blk, :] = jnp.broadcast_to(
                (lane == bias_lane0[h] + nb).astype(BF16), (blk, LANES))
        km = kmean_ref[...]
        pieces = []
        for h in range(heads):
            rest = jnp.where(in_head[h], km, 0.0)
            for _ in range(3):
                part = rest.astype(BF16)
                pieces.append(part)
                rest = rest - part.astype(F32)
        km_parts = jnp.concatenate(pieces, axis=0)
        jidx = lax.broadcasted_iota(jnp.int32, (nb, blk), 0)
        gates = []
        for iq in range(nb):
            g_all = _dot_nt(km_parts, q_ref[0, iq * blk:(iq + 1) * blk, :])
            gates.append([g_all[(3 * h) * nb:(3 * h + 1) * nb] + g_all[(3 * h + 1) * nb:(3 * h + 2) * nb]
                          + g_all[(3 * h + 2) * nb:(3 * h + 3) * nb] for h in range(heads)])
        pad_rows = dh - nb
        for iq in range(nb):
            bias = []
            for h in range(heads):
                gate = jnp.where(jidx < iq, gates[iq][h], -jnp.inf)
                rank = jnp.zeros((nb, blk), jnp.int32)
                for jp in range(iq):
                    other = gate[jp:jp + 1, :]
                    beats = (other > gate) | ((other == gate) & (jp < jidx))
                    rank = rank + beats.astype(jnp.int32)
                keep = (rank < MB_TOPK) & (jidx < iq)
                bias.append(jnp.where(keep, 0.0, MASK_VALUE))
            bias_rows = []
            for h in range(heads):
                other = (h + 1) % heads
                bias_rows += [bias[other], jnp.full((1, blk), MASK_VALUE, F32), jnp.zeros((pad_rows - 1, blk), F32)]
            bias_t = jnp.concatenate(bias_rows, axis=0).T.astype(BF16)
            q_blk = q_ref[0, iq * blk:(iq + 1) * blk, :]
            for h in range(heads):
                qaug_ref[h, iq * blk:(iq + 1) * blk, :] = jnp.where(in_head[h], q_blk, bias_t)

    q_aug = [qaug_ref[h, pl.ds(pl.multiple_of(a * qw, qw), qw), :] for h in range(heads)]
    ones_rows = jnp.ones((BF16_SUBLANES, blk), BF16)

    def v_ext(jb, h):
        return jnp.concatenate([vt_ref[0, jb, h * dh:(h + 1) * dh, :], ones_rows], axis=0)

    def score_dots(t):
        tiles = []
        for h in range(heads):
            for g in range(KV_GROUP):
                j = t * KV_GROUP + g
                kb = jnp.where(j < i, j, nb)
                tiles.append(_dot_nt(kaug_ref[h, pl.ds(pl.multiple_of(kb * blk, blk), blk), :], q_aug[h]))
        return tiles

    def stash(tiles):
        maxima = []
        for h in range(heads):
            cm = None
            for g in range(KV_GROUP):
                st = tiles[h * KV_GROUP + g]
                s_ref[h * KV_GROUP + g] = st
                gm = jnp.max(st, axis=0, keepdims=True)
                cm = gm if cm is None else jnp.maximum(cm, gm)
            maxima.append(cm)
        return maxima

    q_row = lax.broadcasted_iota(jnp.int32, (qw, 1), 0)
    own_tiles = []
    for h in range(heads):
        plain = jnp.where(in_head[h], q_aug[h], jnp.zeros_like(q_aug[h]))
        for d in range(Q_BLOCKS):
            kd = kaug_ref[h, pl.ds(pl.multiple_of((i + d) * blk, blk), blk), :]
            q_own = jnp.where(q_row < (d + 1) * blk, plain, q_aug[h])
            own_tiles.append(_dot_nt(kd, q_own[d * blk:qw]))
    tiles0 = score_dots(0)
    ms, ps = [], []
    for h in range(heads):
        st = []
        m = None
        for d in range(Q_BLOCKS):
            width = qw - d * blk
            key_pos = lax.broadcasted_iota(jnp.int32, (blk, width), 0)
            rel = lax.broadcasted_iota(jnp.int32, (blk, width), 1)
            visible = (rel >= blk) | (key_pos <= rel)
            st.append(jnp.where(visible, own_tiles[h * Q_BLOCKS + d], MASK_VALUE))
            md = jnp.max(st[d], axis=0, keepdims=True)
            if d > 0:
                md = jnp.concatenate([jnp.full((1, d * blk), MASK_VALUE, F32), md], axis=1)
            m = md if m is None else jnp.maximum(m, md)
        ms.append(m)
        ps.append([jnp.exp2(st[d] - m[:, d * blk:qw]).astype(BF16) for d in range(Q_BLOCKS)])
    cms = stash(tiles0)
    carry = []
    for h in range(heads):
        acc = None
        for d in range(Q_BLOCKS):
            term = _dot(v_ext(i + d, h), ps[h][d])
            if d > 0:
                term = jnp.concatenate([jnp.zeros((dh + BF16_SUBLANES, d * blk), F32), term], axis=1)
            acc = term if acc is None else acc + term
        carry += [ms[h], acc, cms[h]]

    def body(t, carry):
        tiles_next = score_dots(t + 1)
        new_m, alphas, ps = [], [], []
        for h in range(heads):
            m, _, cm = carry[3 * h:3 * h + 3]
            m_new = jnp.maximum(m, cm)
            new_m.append(m_new)
            alphas.append(jnp.exp2(m - m_new))
            ps.append([jnp.exp2(s_ref[h * KV_GROUP + g] - m_new).astype(BF16) for g in range(KV_GROUP)])
        cm_next = stash(tiles_next)
        out = []
        for h in range(heads):
            pv = None
            for g in range(KV_GROUP):
                vb = jnp.minimum(t * KV_GROUP + g, nb - 1)
                term = _dot(v_ext(vb, h), ps[h][g])
                pv = term if pv is None else pv + term
            out += [new_m[h], alphas[h] * carry[3 * h + 1] + pv, cm_next[h]]
        return tuple(out)

    carry = lax.fori_loop(0, (i + KV_GROUP - 1) // KV_GROUP, body, tuple(carry))
    ot = jnp.concatenate([carry[3 * h + 1][0:dh] / carry[3 * h + 1][dh:dh + 1] for h in range(heads)], axis=0)
    o = ot.T
    o_ref[0] = (o * _silu(z_ref[0].astype(F32))).astype(o_ref.dtype)


def _moba(qk, vt, z):
    b, s, _ = qk.shape
    nb = s // MB_BLOCK
    pairs = MB_WIDTH // LANES
    heads = LANES // MB_HEAD_DIM
    assert nb + 1 <= MB_HEAD_DIM
    assert nb % Q_BLOCKS == 0
    q_rows = Q_BLOCKS * MB_BLOCK
    return pl.pallas_call(
        _moba_kernel,
        grid=(b, pairs, nb // Q_BLOCKS),
        in_specs=[
            pl.BlockSpec((1, s, LANES), lambda bi, hp, i: (bi, 0, hp)),
            pl.BlockSpec((1, s, LANES), lambda bi, hp, i: (bi, 0, pairs + hp)),
            pl.BlockSpec((1, nb, LANES, MB_BLOCK), lambda bi, hp, i: (bi, 0, hp, 0)),
            pl.BlockSpec((1, q_rows, LANES), lambda bi, hp, i: (bi, i, hp)),
        ],
        out_specs=pl.BlockSpec((1, q_rows, LANES), lambda bi, hp, i: (bi, i, hp)),
        out_shape=jax.ShapeDtypeStruct((b, s, MB_WIDTH), BF16),
        scratch_shapes=[
            pltpu.VMEM((nb, LANES), F32),
            pltpu.VMEM((heads, s + MB_BLOCK, LANES), BF16),
            pltpu.VMEM((heads, s, LANES), BF16),
            pltpu.VMEM((heads * KV_GROUP, MB_BLOCK, q_rows), F32),
        ],
        compiler_params=pltpu.CompilerParams(
            dimension_semantics=("parallel", "parallel", "arbitrary"),
            vmem_limit_bytes=V7X_VMEM_LIMIT_BYTES),
        name="moba_attention",
    )(qk, qk, vt, z)


def _outproj_kernel(x_ref, odn_ref, omb_ref, merge_ref, mod_ref, gpost_ref,
                    wdn_ref, wmb_ref, wout_ref, o_ref, *, d_model):
    y_dn = _dot(odn_ref[0], wdn_ref[...])
    y_mb = _dot(omb_ref[0], wmb_ref[...])
    gates = jax.nn.sigmoid(merge_ref[0].astype(F32))
    mixed_in = gates[:, 0:d_model] * y_dn + gates[:, d_model:2 * d_model] * y_mb
    mixed = _dot(mixed_in.astype(BF16), wout_ref[...])
    y = mixed * lax.rsqrt(jnp.mean(mixed * mixed, axis=-1, keepdims=True) + NORM_EPS) * gpost_ref[...]
    gate = mod_ref[0, :, 2 * d_model:3 * d_model]
    o_ref[0] = x_ref[0] + gate * y


def _out_projection(x, o_dn, o_mb, merge, mod, g_post, w_dn, w_mb, w_out):
    b, s, d = x.shape
    tm = ROW_TILE
    row = lambda bi, ti: (bi, ti, 0)
    const = lambda bi, ti: (0, 0)
    kern = functools.partial(_outproj_kernel, d_model=d)
    return pl.pallas_call(
        kern,
        grid=(b, s // tm),
        in_specs=[
            pl.BlockSpec((1, tm, d), row),
            pl.BlockSpec((1, tm, DN_WIDTH), row),
            pl.BlockSpec((1, tm, MB_WIDTH), row),
            pl.BlockSpec((1, tm, 2 * d), row),
            pl.BlockSpec((1, 1, 3 * d), lambda bi, ti: (bi, 0, 0)),
            pl.BlockSpec((1, d), const),
            pl.BlockSpec((DN_WIDTH, d), const),
            pl.BlockSpec((MB_WIDTH, d), const),
            pl.BlockSpec((d, d), const),
        ],
        out_specs=pl.BlockSpec((1, tm, d), row),
        out_shape=jax.ShapeDtypeStruct((b, s, d), F32),
        compiler_params=pltpu.CompilerParams(
            dimension_semantics=("parallel", "parallel"),
            vmem_limit_bytes=V7X_VMEM_LIMIT_BYTES),
        name="out_projection",
    )(x, o_dn, o_mb, merge, mod, g_post, w_dn, w_mb, w_out)


def _split_w_in(w_in, d_model):
    sizes = (3 * DN_WIDTH, DN_WIDTH, DN_HEADS, DN_HEADS, 3 * MB_WIDTH, MB_WIDTH, 2 * d_model)
    cuts = [0]
    for n in sizes:
        cuts.append(cuts[-1] + n)
    w_qkv_dn, w_z_dn, w_beta, w_a, w_qkv_mb, w_z_mb, w_merge = (
        w_in[:, :, cuts[i]:cuts[i + 1]] for i in range(len(sizes)))
    w_ba = jnp.concatenate([w_beta, w_a], axis=2)
    w_ba_pad = jnp.pad(w_ba, ((0, 0), (0, 0), (0, LANES - 2 * DN_HEADS)))
    w_q_mb = w_qkv_mb[:, :, 0:MB_WIDTH] * (MB_HEAD_DIM ** -0.5 * LOG2_E)
    w_k_mb = w_qkv_mb[:, :, MB_WIDTH:2 * MB_WIDTH]
    w_tok = jnp.concatenate(
        [w_qkv_dn, w_z_dn, w_q_mb, w_k_mb, w_z_mb, w_merge, w_ba_pad], axis=2).astype(BF16)
    w_vt = jnp.swapaxes(w_qkv_mb[:, :, 2 * MB_WIDTH:3 * MB_WIDTH], 1, 2).astype(BF16)
    w_bat = jnp.swapaxes(w_ba, 1, 2).astype(BF16)
    return w_tok, w_vt, w_bat


def _layer(x, mod, g_pre, g_post, w_in_groups, layer, conv_w, a_log, dt_bias, dn_norm_g,
           w_proj_dn, w_proj_mb, w_out):
    b, s, d = x.shape
    w_tok, w_vt, w_bat = w_in_groups
    qkv_dn, z_dn, qk_mb, z_mb, merge, ba, vt, bat = _in_projection(
        x, mod, g_pre.reshape(1, d), w_tok, w_vt, w_bat, conv_w.astype(F32), layer)

    zeros_h = jnp.zeros((DN_HEADS,), F32)
    a_vec = jnp.concatenate([zeros_h, a_log.astype(F32)])
    dt_vec = jnp.concatenate([zeros_h, dt_bias.astype(F32)])
    p_col = jnp.pad(jnp.stack([a_vec, dt_vec]), ((0, 0), (0, LANES - 2 * DN_HEADS)))
    p_row = jnp.stack([a_vec, dt_vec], axis=1)
    o_dn = _deltanet(qkv_dn, ba, bat, z_dn, p_col, p_row, dn_norm_g.reshape(1, DN_HEAD_DIM).astype(F32))
    o_mb = _moba(qk_mb, vt, z_mb)
    return _out_projection(x, o_dn, o_mb, merge, mod, g_post.reshape(1, d),
                           w_proj_dn.astype(BF16), w_proj_mb.astype(BF16), w_out.astype(BF16))


def kernel(x, c, w_ada, b_ada, g_pre, g_post, w_in, conv_w, a_log, dt_bias, dn_norm_g, w_proj_dn, w_proj_mb, w_out):
    depth = w_ada.shape[0]
    b, s, d = x.shape
    assert s % ROW_TILE == 0 and s % MB_BLOCK == 0 and s % DN_TILE == 0
    mod_all = _modulation(c, w_ada, b_ada)
    w_in_groups = _split_w_in(w_in, d)
    for l in range(depth):
        mod = mod_all[l].reshape(b, 1, 3 * d)
        x = _layer(x, mod, g_pre[l], g_post[l], w_in_groups, l, conv_w[l], a_log[l], dt_bias[l],
                   dn_norm_g[l], w_proj_dn[l], w_proj_mb[l], w_out[l])
    return x
```

```python
import functools

import jax
import jax.numpy as jnp
from jax import lax
from jax.experimental import pallas as pl
from jax.experimental.pallas import tpu as pltpu

F32 = jnp.float32
BF16 = jnp.bfloat16

NORM_EPS = 1e-6
DN_HEADS = 4
DN_HEAD_DIM = 128
DN_WIDTH = DN_HEADS * DN_HEAD_DIM
DN_CONV = 4
DN_CHUNK = 64
MB_HEADS = 8
MB_HEAD_DIM = 64
MB_WIDTH = MB_HEADS * MB_HEAD_DIM
MB_BLOCK = 256
MB_TOPK = 3

V7X_VMEM_LIMIT_BYTES = 56 * 1024 * 1024
LANES = 128
CONV_HALO = 8

ROW_TILE = 512
CONV_ROWS = 128
DN_TILE = 256
DN_BATCH = 4
KV_GROUP = 2
Q_BLOCKS = 2
BF16_SUBLANES = 16
MASK_VALUE = -1e30
LOG2_E = 1.4426950408889634

NT_DIMS = (((1,), (1,)), ((), ()))


def _silu(v):
    return v * jax.nn.sigmoid(v)


def _dot(a, b):
    return jnp.dot(a, b, preferred_element_type=F32)


def _dot_nt(a, b):
    return lax.dot_general(a, b, NT_DIMS, preferred_element_type=F32)


def _split_bf16(v):
    hi = v.astype(BF16)
    lo = (v - hi.astype(F32)).astype(BF16)
    return hi, lo


def _mod_kernel(c_ref, w_ref, b_ref, o_ref):
    sc = _silu(c_ref[...])
    o_ref[0] = jnp.dot(sc, w_ref[0], preferred_element_type=F32,
                       precision=lax.Precision.HIGHEST) + b_ref[0]


def _modulation(c, w_ada, b_ada):
    depth, d, d3 = w_ada.shape
    b = c.shape[0]
    tn = 512
    return pl.pallas_call(
        _mod_kernel,
        grid=(depth, d3 // tn),
        in_specs=[
            pl.BlockSpec((b, d), lambda l, j: (0, 0)),
            pl.BlockSpec((1, d, tn), lambda l, j: (l, 0, j)),
            pl.BlockSpec((1, 1, tn), lambda l, j: (l, 0, j)),
        ],
        out_specs=pl.BlockSpec((1, b, tn), lambda l, j: (l, 0, j)),
        out_shape=jax.ShapeDtypeStruct((depth, b, d3), F32),
        name="adaln_modulation",
    )(c, w_ada, b_ada.reshape(depth, 1, d3))


def _inproj_kernel(x_ref, mod_ref, gpre_ref, wtok_ref, wvt_ref, wbat_ref, convw_ref,
                   qkv_dn_ref, z_dn_ref, qk_mb_ref, z_mb_ref, merge_ref, ba_ref, vt_ref, bat_ref,
                   stage_ref, hist_ref, *, d_model):
    ti = pl.program_id(1)
    tm = x_ref.shape[1]
    dk = DN_HEAD_DIM

    @pl.when(ti == 0)
    def _():
        hist_ref[...] = jnp.zeros(hist_ref.shape, F32)

    x = x_ref[0]
    shift = mod_ref[0, :, 0:d_model]
    scale = mod_ref[0, :, d_model:2 * d_model]
    xn = x * lax.rsqrt(jnp.mean(x * x, axis=-1, keepdims=True) + NORM_EPS)
    h = (xn * gpre_ref[...]) * (1.0 + scale) + shift
    hb = h.astype(BF16)

    def conv_part(part):
        c0 = part * DN_WIDTH
        for hd in range(DN_HEADS):
            l0 = hd * dk
            for t0 in range(0, tm, CONV_ROWS):
                xs = stage_ref[part, t0:t0 + CONV_HALO + CONV_ROWS, l0:l0 + dk]
                acc = convw_ref[0:1, c0 + l0:c0 + l0 + dk] * xs
                for k in range(1, DN_CONV):
                    acc = pltpu.roll(acc, 1, 0) + convw_ref[k:k + 1, c0 + l0:c0 + l0 + dk] * xs
                y = _silu(acc[CONV_HALO:CONV_HALO + CONV_ROWS])
                if part < 2:
                    inv = lax.rsqrt(jnp.sum(y * y, axis=-1, keepdims=True) + NORM_EPS)
                    y = y * (inv * (dk ** -0.5) if part == 0 else inv)
                qkv_dn_ref[0, t0:t0 + CONV_ROWS, c0 + l0:c0 + l0 + dk] = y.astype(qkv_dn_ref.dtype)

    plain = []
    col = 3 * DN_WIDTH
    for ref in (z_dn_ref, qk_mb_ref, z_mb_ref, merge_ref, ba_ref):
        width = ref.shape[-1]
        step = min(width, 512)
        plain += [(ref, c0, step, col + c0) for c0 in range(0, width, step)]
        col += width
    per_part = -(-len(plain) // 3)
    for part in range(3):
        c0 = part * DN_WIDTH
        stage_ref[part, 0:CONV_HALO, :] = hist_ref[part]
        stage_ref[part, CONV_HALO:CONV_HALO + tm, :] = _dot(hb, wtok_ref[0, :, c0:c0 + DN_WIDTH])
        hist_ref[part] = stage_ref[part, tm:tm + CONV_HALO, :]
        for ref, r0, step, wcol in plain[part * per_part:(part + 1) * per_part]:
            ref[0, :, r0:r0 + step] = _dot(hb, wtok_ref[0, :, wcol:wcol + step]).astype(ref.dtype)
        conv_part(part)

    vt = _dot_nt(wvt_ref[0], hb).astype(BF16)
    for j in range(vt_ref.shape[1]):
        vt_ref[0, j] = vt[:, j * MB_BLOCK:(j + 1) * MB_BLOCK]
    bat_ref[0] = _dot_nt(wbat_ref[0], hb)


def _in_projection(x, mod, g_pre, w_tok, w_vt, w_bat, conv_w, layer):
    b, s, d = x.shape
    tm = ROW_TILE
    n_tok = w_tok.shape[2]
    this_layer = lambda bi, ti: (layer, 0, 0)
    kern = functools.partial(_inproj_kernel, d_model=d)
    const = lambda bi, ti: (0, 0)
    row = lambda bi, ti: (bi, ti, 0)
    widths = [3 * DN_WIDTH, DN_WIDTH, 2 * MB_WIDTH, MB_WIDTH, 2 * d, LANES]
    dtypes = [BF16, BF16, BF16, BF16, BF16, F32]
    out_shape = [jax.ShapeDtypeStruct((b, s, w), t) for w, t in zip(widths, dtypes)]
    out_specs = [pl.BlockSpec((1, tm, w), row) for w in widths]
    out_shape += [
        jax.ShapeDtypeStruct((b, s // MB_BLOCK, MB_WIDTH, MB_BLOCK), BF16),
        jax.ShapeDtypeStruct((b, 2 * DN_HEADS, s), F32),
    ]
    out_specs += [
        pl.BlockSpec((1, tm // MB_BLOCK, MB_WIDTH, MB_BLOCK), lambda bi, ti: (bi, ti, 0, 0)),
        pl.BlockSpec((1, 2 * DN_HEADS, tm), lambda bi, ti: (bi, 0, ti)),
    ]
    return pl.pallas_call(
        kern,
        grid=(b, s // tm),
        in_specs=[
            pl.BlockSpec((1, tm, d), row),
            pl.BlockSpec((1, 1, 3 * d), lambda bi, ti: (bi, 0, 0)),
            pl.BlockSpec((1, d), const),
            pl.BlockSpec((1, d, n_tok), this_layer, pipeline_mode=pl.Buffered(1)),
            pl.BlockSpec((1, MB_WIDTH, d), this_layer, pipeline_mode=pl.Buffered(1)),
            pl.BlockSpec((1, 2 * DN_HEADS, d), this_layer, pipeline_mode=pl.Buffered(1)),
            pl.BlockSpec((DN_CONV, 3 * DN_WIDTH), const),
        ],
        out_specs=out_specs,
        out_shape=out_shape,
        scratch_shapes=[
            pltpu.VMEM((3, CONV_HALO + tm, DN_WIDTH), F32),
            pltpu.VMEM((3, CONV_HALO, DN_WIDTH), F32),
        ],
        compiler_params=pltpu.CompilerParams(
            dimension_semantics=("parallel", "arbitrary"),
            vmem_limit_bytes=V7X_VMEM_LIMIT_BYTES),
        name="in_projection",
    )(x, mod, g_pre, w_tok, w_vt, w_bat, conv_w)


def _dn_kernel(qkv_ref, ba_ref, bat_ref, z_ref, pcol_ref, prow_ref, gnorm_ref, o_ref, state_ref):
    t = pl.program_id(1)
    nbat, tt = qkv_ref.shape[0], qkv_ref.shape[1]
    n_chunks = tt // DN_CHUNK
    dk = DN_HEAD_DIM
    cs = DN_CHUNK

    @pl.when(t == 0)
    def _():
        state_ref[...] = jnp.zeros(state_ref.shape, F32)

    ri = lax.broadcasted_iota(jnp.int32, (tt, tt), 0)
    ci = lax.broadcasted_iota(jnp.int32, (tt, tt), 1)
    same_chunk = (ri // DN_CHUNK) == (ci // DN_CHUNK)
    lower = (same_chunk & (ri >= ci)).astype(BF16)
    upper = (same_chunk & (ri <= ci)).astype(BF16)

    beta_all, gc_row, eg, eg_to_last, eg_last = [], [], [], [], []
    for bi in range(nbat):
        ba = ba_ref[bi]
        beta_all.append(jax.nn.sigmoid(ba))
        g_col = -jnp.exp(pcol_ref[0:1, :]) * jax.nn.softplus(ba + pcol_ref[1:2, :])
        g_row = -jnp.exp(prow_ref[:, 0:1]) * jax.nn.softplus(bat_ref[bi] + prow_ref[:, 1:2])
        gch, gcl = _split_bf16(g_col)
        gc_col = _dot(lower, gch) + _dot(lower, gcl)
        grh, grl = _split_bf16(g_row)
        gc_row.append(_dot(grh, upper) + _dot(grl, upper))
        eg.append(jnp.exp(gc_col))
        g_last = [gc_col[(c + 1) * DN_CHUNK - 1:(c + 1) * DN_CHUNK, :] for c in range(n_chunks)]
        eg_to_last.append(jnp.concatenate(
            [jnp.exp(g_last[c] - gc_col[c * DN_CHUNK:(c + 1) * DN_CHUNK, :]) for c in range(n_chunks)], axis=0))
        eg_last.append([jnp.exp(g_last[c]) for c in range(n_chunks)])

    col_chunk = lax.broadcasted_iota(jnp.int32, (dk, tt), 1) // cs

    wr = lax.broadcasted_iota(jnp.int32, (cs, tt), 0)
    wl = lax.broadcasted_iota(jnp.int32, (cs, tt), 1)
    wj = wl % cs
    wc = wl // cs
    causal_w = wr >= wj
    strict_w = wr > wj
    same_chunk_b = same_chunk.astype(BF16)

    def fold(x):
        acc = None
        for c in range(n_chunks):
            part = jnp.where(wc == c, x[c * cs:(c + 1) * cs, :], 0.0)
            acc = part if acc is None else acc + part
        return acc

    def block_diag(xw):
        return jnp.where(same_chunk, jnp.concatenate([xw] * n_chunks, axis=0), jnp.zeros((), xw.dtype))

    def blocks_w(n):
        return (wr // n) == (wj // n)

    heads = range(nbat * DN_HEADS)
    bat = [u // DN_HEADS for u in heads]
    hd = [u % DN_HEADS for u in heads]
    gl = [DN_HEADS + hd[h] for h in heads]
    q_b = [qkv_ref[bat[h], :, hd[h] * dk:(hd[h] + 1) * dk] for h in heads]
    kn_b = [qkv_ref[bat[h], :, DN_WIDTH + hd[h] * dk:DN_WIDTH + (hd[h] + 1) * dk] for h in heads]
    qn = [q_b[h].astype(F32) for h in heads]
    kn = [kn_b[h].astype(F32) for h in heads]
    beta = [beta_all[bat[h]][:, hd[h]:hd[h] + 1] for h in heads]
    kb = [kn[h] * beta[h] for h in heads]
    vb = [qkv_ref[bat[h], :, 2 * DN_WIDTH + hd[h] * dk:2 * DN_WIDTH + (hd[h] + 1) * dk].astype(F32) * beta[h]
          for h in heads]
    scores = [_dot_nt(jnp.concatenate([kb[h].astype(BF16), q_b[h]], axis=0), kn_b[h]) for h in heads]
    g_rows = [gc_row[bat[h]][gl[h]:gl[h] + 1, :] for h in heads]
    gcol_w = []
    for h in heads:
        rh, rl = _split_bf16(jnp.where(wj == wr, g_rows[h], 0.0))
        gcol_w.append(_dot(rh, same_chunk_b) + _dot(rl, same_chunk_b))

    a, qk = [], []
    for h in heads:
        diff = gcol_w[h] - g_rows[h]
        decay = jnp.where(causal_w, jnp.exp(jnp.where(causal_w, diff, 0.0)), 0.0)
        a.append(jnp.where(strict_w, fold(scores[h][0:tt]) * decay, 0.0))
        qk.append(fold(scores[h][tt:2 * tt]) * decay)

    p = [(wr == wj).astype(F32) - jnp.where(blocks_w(2), a[h], 0.0) for h in heads]
    n = 2
    while n < cs:
        lower_left = blocks_w(2 * n) & jnp.logical_not(blocks_w(n))
        p_b = [p[h].astype(BF16) for h in heads]
        mid = [_dot(jnp.where(lower_left, a[h], 0.0).astype(BF16), block_diag(p_b[h])) for h in heads]
        upd = [_dot(p_b[h], block_diag(mid[h].astype(BF16))) for h in heads]
        p = [p[h] - upd[h] for h in heads]
        n *= 2
    p = [block_diag(p[h].astype(BF16)) for h in heads]

    eg_h = [eg[bat[h]][:, gl[h]:gl[h] + 1] for h in heads]
    uw = [_dot(p[h], jnp.concatenate([vb[h].astype(BF16), (kb[h] * eg_h[h]).astype(BF16)], axis=1))
          for h in heads]
    u = [uw[h][:, 0:dk] for h in heads]
    w_b = [uw[h][:, dk:2 * dk].astype(BF16) for h in heads]
    qg_b = [(qn[h] * eg_h[h]).astype(BF16) for h in heads]
    kdt = [(kn[h] * eg_to_last[bat[h]][:, gl[h]:gl[h] + 1]).T for h in heads]

    state = [state_ref[h] for h in heads]
    v_rows = [[] for _ in heads]
    o_rows = [[] for _ in heads]
    for c in range(n_chunks):
        rows = slice(c * DN_CHUNK, (c + 1) * DN_CHUNK)
        r2 = [_dot(jnp.concatenate([w_b[h][rows], qg_b[h][rows]], axis=0), state[h].astype(BF16)) for h in heads]
        lhs3 = []
        for h in heads:
            kdt_c = jnp.where(col_chunk == c, kdt[h], 0.0).astype(BF16)
            qk_c = jnp.where(wc == c, qk[h], 0.0).astype(BF16)
            lhs3.append(jnp.concatenate([qk_c, kdt_c], axis=0))
        r3 = []
        for h in heads:
            v_rows[h].append((u[h][rows] - r2[h][0:DN_CHUNK]).astype(BF16))
            v_blk = jnp.concatenate(
                v_rows[h] + [jnp.zeros(((n_chunks - 1 - c) * DN_CHUNK, dk), BF16)] * (c < n_chunks - 1), axis=0)
            r3.append(_dot(lhs3[h], v_blk))
        for h in heads:
            o_rows[h].append(r2[h][DN_CHUNK:2 * DN_CHUNK] + r3[h][0:DN_CHUNK])
            state[h] = state[h] * eg_last[bat[h]][c][:, gl[h]:gl[h] + 1] + r3[h][DN_CHUNK:DN_CHUNK + dk]

    for h in heads:
        state_ref[h] = state[h]
        o = jnp.concatenate(o_rows[h], axis=0)
        on = o * lax.rsqrt(jnp.mean(o * o, axis=-1, keepdims=True) + NORM_EPS) * gnorm_ref[...]
        zh = z_ref[bat[h], :, hd[h] * dk:(hd[h] + 1) * dk].astype(F32)
        o_ref[bat[h], :, hd[h] * dk:(hd[h] + 1) * dk] = (on * _silu(zh)).astype(o_ref.dtype)


def _deltanet(qkv, ba, bat, z, p_col, p_row, g_norm):
    b, s, w3 = qkv.shape
    tt = DN_TILE
    nbat = DN_BATCH if b % DN_BATCH == 0 else 1
    row = lambda bi, ti: (bi, ti, 0)
    const = lambda bi, ti: (0, 0)
    return pl.pallas_call(
        _dn_kernel,
        grid=(b // nbat, s // tt),
        in_specs=[
            pl.BlockSpec((nbat, tt, w3), row),
            pl.BlockSpec((nbat, tt, LANES), row),
            pl.BlockSpec((nbat, 2 * DN_HEADS, tt), lambda bi, ti: (bi, 0, ti)),
            pl.BlockSpec((nbat, tt, DN_WIDTH), row),
            pl.BlockSpec((2, LANES), const),
            pl.BlockSpec((2 * DN_HEADS, 2), const),
            pl.BlockSpec((1, DN_HEAD_DIM), const),
        ],
        out_specs=pl.BlockSpec((nbat, tt, DN_WIDTH), row),
        out_shape=jax.ShapeDtypeStruct((b, s, DN_WIDTH), BF16),
        scratch_shapes=[
            pltpu.VMEM((nbat * DN_HEADS, DN_HEAD_DIM, DN_HEAD_DIM), F32),
        ],
        compiler_params=pltpu.CompilerParams(
            dimension_semantics=("parallel", "arbitrary"),
            vmem_limit_bytes=V7X_VMEM_LIMIT_BYTES),
        name="deltanet",
    )(qkv, ba, bat, z, p_col, p_row, g_norm)


def _moba_kernel(q_ref, k_ref, vt_ref, z_ref, o_ref, kmean_ref, kaug_ref, qaug_ref, s_ref):
    a = pl.program_id(2)
    i = a * Q_BLOCKS
    nb = vt_ref.shape[1]
    blk = MB_BLOCK
    qw = Q_BLOCKS * blk
    dh = MB_HEAD_DIM
    heads = LANES // dh
    lane = lax.broadcasted_iota(jnp.int32, (1, LANES), 1)
    in_head = [(lane >= h * dh) & (lane < (h + 1) * dh) for h in range(heads)]
    bias_lane0 = [((h + 1) % heads) * dh for h in range(heads)]

    @pl.when(a == 0)
    def _():
        for j in range(nb):
            kj = k_ref[0, j * blk:(j + 1) * blk, :]
            kmean_ref[j:j + 1, :] = jnp.mean(kj.astype(F32), axis=0, keepdims=True)
            for h in range(heads):
                onehot = (lane == bias_lane0[h] + j).astype(BF16)
                kaug_ref[h, j * blk:(j + 1) * blk, :] = jnp.where(in_head[h], kj, onehot)
        for h in range(heads):
            kaug_ref[h, nb * blk:(nb + 1) * blk, :] = jnp.broadcast_to(
                (lane == bias_lane0[h] + nb).astype(BF16), (blk, LANES))
        km = kmean_ref[...]
        pieces = []
        for h in range(heads):
            rest = jnp.where(in_head[h], km, 0.0)
            for _ in range(3):
                part = rest.astype(BF16)
                pieces.append(part)
                rest = rest - part.astype(F32)
        km_parts = jnp.concatenate(pieces, axis=0)
        jidx = lax.broadcasted_iota(jnp.int32, (nb, blk), 0)
        gates = []
        for iq in range(nb):
            g_all = _dot_nt(km_parts, q_ref[0, iq * blk:(iq + 1) * blk, :])
            gates.append([g_all[(3 * h) * nb:(3 * h + 1) * nb] + g_all[(3 * h + 1) * nb:(3 * h + 2) * nb]
                          + g_all[(3 * h + 2) * nb:(3 * h + 3) * nb] for h in range(heads)])
        pad_rows = dh - nb
        for iq in range(nb):
            bias = []
            for h in range(heads):
                gate = jnp.where(jidx < iq, gates[iq][h], -jnp.inf)
                rank = jnp.zeros((nb, blk), jnp.int32)
                for jp in range(iq):
                    other = gate[jp:jp + 1, :]
                    beats = (other > gate) | ((other == gate) & (jp < jidx))
                    rank = rank + beats.astype(jnp.int32)
                keep = (rank < MB_TOPK) & (jidx < iq)
                bias.append(jnp.where(keep, 0.0, MASK_VALUE))
            bias_rows = []
            for h in range(heads):
                other = (h + 1) % heads
                bias_rows += [bias[other], jnp.full((1, blk), MASK_VALUE, F32), jnp.zeros((pad_rows - 1, blk), F32)]
            bias_t = jnp.concatenate(bias_rows, axis=0).T.astype(BF16)
            q_blk = q_ref[0, iq * blk:(iq + 1) * blk, :]
            for h in range(heads):
                qaug_ref[h, iq * blk:(iq + 1) * blk, :] = jnp.where(in_head[h], q_blk, bias_t)

    q_aug = [qaug_ref[h, pl.ds(pl.multiple_of(a * qw, qw), qw), :] for h in range(heads)]
    ones_rows = jnp.ones((BF16_SUBLANES, blk), BF16)

    def v_ext(jb, h):
        return jnp.concatenate([vt_ref[0, jb, h * dh:(h + 1) * dh, :], ones_rows], axis=0)

    def score_dots(t):
        tiles = []
        for h in range(heads):
            for g in range(KV_GROUP):
                j = t * KV_GROUP + g
                kb = jnp.where(j < i, j, nb)
                tiles.append(_dot_nt(kaug_ref[h, pl.ds(pl.multiple_of(kb * blk, blk), blk), :], q_aug[h]))
        return tiles

    def stash(tiles):
        maxima = []
        for h in range(heads):
            cm = None
            for g in range(KV_GROUP):
                st = tiles[h * KV_GROUP + g]
                s_ref[h * KV_GROUP + g] = st
                gm = jnp.max(st, axis=0, keepdims=True)
                cm = gm if cm is None else jnp.maximum(cm, gm)
            maxima.append(cm)
        return maxima

    q_row = lax.broadcasted_iota(jnp.int32, (qw, 1), 0)
    own_tiles = []
    for h in range(heads):
        plain = jnp.where(in_head[h], q_aug[h], jnp.zeros_like(q_aug[h]))
        for d in range(Q_BLOCKS):
            kd = kaug_ref[h, pl.ds(pl.multiple_of((i + d) * blk, blk), blk), :]
            q_own = jnp.where(q_row < (d + 1) * blk, plain, q_aug[h])
            own_tiles.append(_dot_nt(kd, q_own[d * blk:qw]))
    tiles0 = score_dots(0)
    ms, ps = [], []
    for h in range(heads):
        st = []
        m = None
        for d in range(Q_BLOCKS):
            width = qw - d * blk
            key_pos = lax.broadcasted_iota(jnp.int32, (blk, width), 0)
            rel = lax.broadcasted_iota(jnp.int32, (blk, width), 1)
            visible = (rel >= blk) | (key_pos <= rel)
            st.append(jnp.where(visible, own_tiles[h * Q_BLOCKS + d], MASK_VALUE))
            md = jnp.max(st[d], axis=0, keepdims=True)
            if d > 0:
                md = jnp.concatenate([jnp.full((1, d * blk), MASK_VALUE, F32), md], axis=1)
            m = md if m is None else jnp.maximum(m, md)
        ms.append(m)
        ps.append([jnp.exp2(st[d] - m[:, d * blk:qw]).astype(BF16) for d in range(Q_BLOCKS)])
    cms = stash(tiles0)
    carry = []
    for h in range(heads):
        acc = None
        for d in range(Q_BLOCKS):
            term = _dot(v_ext(i + d, h), ps[h][d])
            if d > 0:
                term = jnp.concatenate([jnp.zeros((dh + BF16_SUBLANES, d * blk), F32), term], axis=1)
            acc = term if acc is None else acc + term
        carry += [ms[h], acc, cms[h]]

    def body(t, carry):
        tiles_next = score_dots(t + 1)
        new_m, alphas, ps = [], [], []
        for h in range(heads):
            m, _, cm = carry[3 * h:3 * h + 3]
            m_new = jnp.maximum(m, cm)
            new_m.append(m_new)
            alphas.append(jnp.exp2(m - m_new))
            ps.append([jnp.exp2(s_ref[h * KV_GROUP + g] - m_new).astype(BF16) for g in range(KV_GROUP)])
        cm_next = stash(tiles_next)
        out = []
        for h in range(heads):
            pv = None
            for g in range(KV_GROUP):
                vb = jnp.minimum(t * KV_GROUP + g, nb - 1)
                term = _dot(v_ext(vb, h), ps[h][g])
                pv = term if pv is None else pv + term
            out += [new_m[h], alphas[h] * carry[3 * h + 1] + pv, cm_next[h]]
        return tuple(out)

    carry = lax.fori_loop(0, (i + KV_GROUP - 1) // KV_GROUP, body, tuple(carry))
    ot = jnp.concatenate([carry[3 * h + 1][0:dh] / carry[3 * h + 1][dh:dh + 1] for h in range(heads)], axis=0)
    o = ot.T
    o_ref[0] = (o * _silu(z_ref[0].astype(F32))).astype(o_ref.dtype)


def _moba(qk, vt, z):
    b, s, _ = qk.shape
    nb = s // MB_BLOCK
    pairs = MB_WIDTH // LANES
    heads = LANES // MB_HEAD_DIM
    assert nb + 1 <= MB_HEAD_DIM
    assert nb % Q_BLOCKS == 0
    q_rows = Q_BLOCKS * MB_BLOCK
    return pl.pallas_call(
        _moba_kernel,
        grid=(b, pairs, nb // Q_BLOCKS),
        in_specs=[
            pl.BlockSpec((1, s, LANES), lambda bi, hp, i: (bi, 0, hp)),
            pl.BlockSpec((1, s, LANES), lambda bi, hp, i: (bi, 0, pairs + hp)),
            pl.BlockSpec((1, nb, LANES, MB_BLOCK), lambda bi, hp, i: (bi, 0, hp, 0)),
            pl.BlockSpec((1, q_rows, LANES), lambda bi, hp, i: (bi, i, hp)),
        ],
        out_specs=pl.BlockSpec((1, q_rows, LANES), lambda bi, hp, i: (bi, i, hp)),
        out_shape=jax.ShapeDtypeStruct((b, s, MB_WIDTH), BF16),
        scratch_shapes=[
            pltpu.VMEM((nb, LANES), F32),
            pltpu.VMEM((heads, s + MB_BLOCK, LANES), BF16),
            pltpu.VMEM((heads, s, LANES), BF16),
            pltpu.VMEM((heads * KV_GROUP, MB_BLOCK, q_rows), F32),
        ],
        compiler_params=pltpu.CompilerParams(
            dimension_semantics=("parallel", "parallel", "arbitrary"),
            vmem_limit_bytes=V7X_VMEM_LIMIT_BYTES),
        name="moba_attention",
    )(qk, qk, vt, z)


def _outproj_kernel(x_ref, odn_ref, omb_ref, merge_ref, mod_ref, gpost_ref,
                    wdn_ref, wmb_ref, wout_ref, o_ref, *, d_model):
    y_dn = _dot(odn_ref[0], wdn_ref[...])
    y_mb = _dot(omb_ref[0], wmb_ref[...])
    gates = jax.nn.sigmoid(merge_ref[0].astype(F32))
    mixed_in = gates[:, 0:d_model] * y_dn + gates[:, d_model:2 * d_model] * y_mb
    mixed = _dot(mixed_in.astype(BF16), wout_ref[...])
    y = mixed * lax.rsqrt(jnp.mean(mixed * mixed, axis=-1, keepdims=True) + NORM_EPS) * gpost_ref[...]
    gate = mod_ref[0, :, 2 * d_model:3 * d_model]
    o_ref[0] = x_ref[0] + gate * y


def _out_projection(x, o_dn, o_mb, merge, mod, g_post, w_dn, w_mb, w_out):
    b, s, d = x.shape
    tm = ROW_TILE
    row = lambda bi, ti: (bi, ti, 0)
    const = lambda bi, ti: (0, 0)
    kern = functools.partial(_outproj_kernel, d_model=d)
    return pl.pallas_call(
        kern,
        grid=(b, s // tm),
        in_specs=[
            pl.BlockSpec((1, tm, d), row),
            pl.BlockSpec((1, tm, DN_WIDTH), row),
            pl.BlockSpec((1, tm, MB_WIDTH), row),
            pl.BlockSpec((1, tm, 2 * d), row),
            pl.BlockSpec((1, 1, 3 * d), lambda bi, ti: (bi, 0, 0)),
            pl.BlockSpec((1, d), const),
            pl.BlockSpec((DN_WIDTH, d), const),
            pl.BlockSpec((MB_WIDTH, d), const),
            pl.BlockSpec((d, d), const),
        ],
        out_specs=pl.BlockSpec((1, tm, d), row),
        out_shape=jax.ShapeDtypeStruct((b, s, d), F32),
        compiler_params=pltpu.CompilerParams(
            dimension_semantics=("parallel", "parallel"),
            vmem_limit_bytes=V7X_VMEM_LIMIT_BYTES),
        name="out_projection",
    )(x, o_dn, o_mb, merge, mod, g_post, w_dn, w_mb, w_out)


def _split_w_in(w_in, d_model):
    sizes = (3 * DN_WIDTH, DN_WIDTH, DN_HEADS, DN_HEADS, 3 * MB_WIDTH, MB_WIDTH, 2 * d_model)
    cuts = [0]
    for n in sizes:
        cuts.append(cuts[-1] + n)
    w_qkv_dn, w_z_dn, w_beta, w_a, w_qkv_mb, w_z_mb, w_merge = (
        w_in[:, :, cuts[i]:cuts[i + 1]] for i in range(len(sizes)))
    w_ba = jnp.concatenate([w_beta, w_a], axis=2)
    w_ba_pad = jnp.pad(w_ba, ((0, 0), (0, 0), (0, LANES - 2 * DN_HEADS)))
    w_q_mb = w_qkv_mb[:, :, 0:MB_WIDTH] * (MB_HEAD_DIM ** -0.5 * LOG2_E)
    w_k_mb = w_qkv_mb[:, :, MB_WIDTH:2 * MB_WIDTH]
    w_tok = jnp.concatenate(
        [w_qkv_dn, w_z_dn, w_q_mb, w_k_mb, w_z_mb, w_merge, w_ba_pad], axis=2).astype(BF16)
    w_vt = jnp.swapaxes(w_qkv_mb[:, :, 2 * MB_WIDTH:3 * MB_WIDTH], 1, 2).astype(BF16)
    w_bat = jnp.swapaxes(w_ba, 1, 2).astype(BF16)
    return w_tok, w_vt, w_bat


def _layer(x, mod, g_pre, g_post, w_in_groups, layer, conv_w, a_log, dt_bias, dn_norm_g,
           w_proj_dn, w_proj_mb, w_out):
    b, s, d = x.shape
    w_tok, w_vt, w_bat = w_in_groups
    qkv_dn, z_dn, qk_mb, z_mb, merge, ba, vt, bat = _in_projection(
        x, mod, g_pre.reshape(1, d), w_tok, w_vt, w_bat, conv_w.astype(F32), layer)

    zeros_h = jnp.zeros((DN_HEADS,), F32)
    a_vec = jnp.concatenate([zeros_h, a_log.astype(F32)])
    dt_vec = jnp.concatenate([zeros_h, dt_bias.astype(F32)])
    p_col = jnp.pad(jnp.stack([a_vec, dt_vec]), ((0, 0), (0, LANES - 2 * DN_HEADS)))
    p_row = jnp.stack([a_vec, dt_vec], axis=1)
    o_dn = _deltanet(qkv_dn, ba, bat, z_dn, p_col, p_row, dn_norm_g.reshape(1, DN_HEAD_DIM).astype(F32))
    o_mb = _moba(qk_mb, vt, z_mb)
    return _out_projection(x, o_dn, o_mb, merge, mod, g_post.reshape(1, d),
                           w_proj_dn.astype(BF16), w_proj_mb.astype(BF16), w_out.astype(BF16))


def kernel(x, c, w_ada, b_ada, g_pre, g_post, w_in, conv_w, a_log, dt_bias, dn_norm_g, w_proj_dn, w_proj_mb, w_out):
    depth = w_ada.shape[0]
    b, s, d = x.shape
    assert s % ROW_TILE == 0 and s % MB_BLOCK == 0 and s % DN_TILE == 0
    mod_all = _modulation(c, w_ada, b_ada)
    w_in_groups = _split_w_in(w_in, d)
    for l in range(depth):
        mod = mod_all[l].reshape(b, 1, 3 * d)
        x = _layer(x, mod, g_pre[l], g_post[l], w_in_groups, l, conv_w[l], a_log[l], dt_bias[l],
                   dn_norm_g[l], w_proj_dn[l], w_proj_mb[l], w_out[l])
    return x
```

```python
import functools

import jax
import jax.numpy as jnp
from jax import lax
from jax.experimental import pallas as pl
from jax.experimental.pallas import tpu as pltpu

F32 = jnp.float32
BF16 = jnp.bfloat16

NORM_EPS = 1e-6
DN_HEADS = 4
DN_HEAD_DIM = 128
DN_WIDTH = DN_HEADS * DN_HEAD_DIM
DN_CONV = 4
DN_CHUNK = 64
MB_HEADS = 8
MB_HEAD_DIM = 64
MB_WIDTH = MB_HEADS * MB_HEAD_DIM
MB_BLOCK = 256
MB_TOPK = 3

V7X_VMEM_LIMIT_BYTES = 56 * 1024 * 1024
LANES = 128
CONV_HALO = 8

ROW_TILE = 512
OUT_TILE = 1024
CONV_ROWS = 128
DN_TILE = 256
DN_BATCH = 4
KV_GROUP = 2
Q_BLOCKS = 2
BF16_SUBLANES = 16
MASK_VALUE = -1e30
LOG2_E = 1.4426950408889634

NT_DIMS = (((1,), (1,)), ((), ()))


def _silu(v):
    return v * jax.nn.sigmoid(v)


def _dot(a, b):
    return jnp.dot(a, b, preferred_element_type=F32)


def _dot_nt(a, b):
    return lax.dot_general(a, b, NT_DIMS, preferred_element_type=F32)


def _split_bf16(v):
    hi = v.astype(BF16)
    lo = (v - hi.astype(F32)).astype(BF16)
    return hi, lo


def _mod_kernel(c_ref, w_ref, b_ref, o_ref):
    sc = _silu(c_ref[...])
    o_ref[0] = jnp.dot(sc, w_ref[0], preferred_element_type=F32,
                       precision=lax.Precision.HIGHEST) + b_ref[0]


def _modulation(c, w_ada, b_ada):
    depth, d, d3 = w_ada.shape
    b = c.shape[0]
    tn = 512
    return pl.pallas_call(
        _mod_kernel,
        grid=(depth, d3 // tn),
        in_specs=[
            pl.BlockSpec((b, d), lambda l, j: (0, 0)),
            pl.BlockSpec((1, d, tn), lambda l, j: (l, 0, j)),
            pl.BlockSpec((1, 1, tn), lambda l, j: (l, 0, j)),
        ],
        out_specs=pl.BlockSpec((1, b, tn), lambda l, j: (l, 0, j)),
        out_shape=jax.ShapeDtypeStruct((depth, b, d3), F32),
        name="adaln_modulation",
    )(c, w_ada, b_ada.reshape(depth, 1, d3))


def _inproj_kernel(x_ref, mod_ref, gpre_ref, wtok_ref, wvt_ref, wbat_ref, convw_ref,
                   qkv_dn_ref, z_dn_ref, qk_mb_ref, z_mb_ref, merge_ref, ba_ref, vt_ref, bat_ref,
                   stage_ref, hist_ref, *, d_model):
    ti = pl.program_id(1)
    tm = x_ref.shape[1]
    dk = DN_HEAD_DIM

    @pl.when(ti == 0)
    def _():
        hist_ref[...] = jnp.zeros(hist_ref.shape, F32)

    x = x_ref[0]
    shift = mod_ref[0, :, 0:d_model]
    scale = mod_ref[0, :, d_model:2 * d_model]
    xn = x * lax.rsqrt(jnp.mean(x * x, axis=-1, keepdims=True) + NORM_EPS)
    h = (xn * gpre_ref[...]) * (1.0 + scale) + shift
    hb = h.astype(BF16)

    def conv_part(part):
        c0 = part * DN_WIDTH
        for hd in range(DN_HEADS):
            l0 = hd * dk
            for t0 in range(0, tm, CONV_ROWS):
                xs = stage_ref[part, t0:t0 + CONV_HALO + CONV_ROWS, l0:l0 + dk]
                acc = convw_ref[0:1, c0 + l0:c0 + l0 + dk] * xs
                for k in range(1, DN_CONV):
                    acc = pltpu.roll(acc, 1, 0) + convw_ref[k:k + 1, c0 + l0:c0 + l0 + dk] * xs
                y = _silu(acc[CONV_HALO:CONV_HALO + CONV_ROWS])
                if part < 2:
                    inv = lax.rsqrt(jnp.sum(y * y, axis=-1, keepdims=True) + NORM_EPS)
                    y = y * (inv * (dk ** -0.5) if part == 0 else inv)
                qkv_dn_ref[0, t0:t0 + CONV_ROWS, c0 + l0:c0 + l0 + dk] = y.astype(qkv_dn_ref.dtype)

    plain = []
    col = 3 * DN_WIDTH
    for ref in (z_dn_ref, qk_mb_ref, z_mb_ref, merge_ref, ba_ref):
        width = ref.shape[-1]
        step = min(width, 512)
        plain += [(ref, c0, step, col + c0) for c0 in range(0, width, step)]
        col += width
    per_part = -(-len(plain) // 3)
    for part in range(3):
        c0 = part * DN_WIDTH
        stage_ref[part, 0:CONV_HALO, :] = hist_ref[part]
        stage_ref[part, CONV_HALO:CONV_HALO + tm, :] = _dot(hb, wtok_ref[0, :, c0:c0 + DN_WIDTH])
        hist_ref[part] = stage_ref[part, tm:tm + CONV_HALO, :]
        for ref, r0, step, wcol in plain[part * per_part:(part + 1) * per_part]:
            ref[0, :, r0:r0 + step] = _dot(hb, wtok_ref[0, :, wcol:wcol + step]).astype(ref.dtype)
        conv_part(part)

    vt = _dot_nt(wvt_ref[0], hb).astype(BF16)
    for j in range(vt_ref.shape[1]):
        vt_ref[0, j] = vt[:, j * MB_BLOCK:(j + 1) * MB_BLOCK]
    bat_ref[0] = _dot_nt(wbat_ref[0], hb)


def _in_projection(x, mod, g_pre, w_tok, w_vt, w_bat, conv_w, layer):
    b, s, d = x.shape
    tm = ROW_TILE
    n_tok = w_tok.shape[2]
    this_layer = lambda bi, ti: (layer, 0, 0)
    kern = functools.partial(_inproj_kernel, d_model=d)
    const = lambda bi, ti: (0, 0)
    row = lambda bi, ti: (bi, ti, 0)
    widths = [3 * DN_WIDTH, DN_WIDTH, 2 * MB_WIDTH, MB_WIDTH, 2 * d, LANES]
    dtypes = [BF16, BF16, BF16, BF16, BF16, F32]
    out_shape = [jax.ShapeDtypeStruct((b, s, w), t) for w, t in zip(widths, dtypes)]
    out_specs = [pl.BlockSpec((1, tm, w), row) for w in widths]
    out_shape += [
        jax.ShapeDtypeStruct((b, s // MB_BLOCK, MB_WIDTH, MB_BLOCK), BF16),
        jax.ShapeDtypeStruct((b, 2 * DN_HEADS, s), F32),
    ]
    out_specs += [
        pl.BlockSpec((1, tm // MB_BLOCK, MB_WIDTH, MB_BLOCK), lambda bi, ti: (bi, ti, 0, 0)),
        pl.BlockSpec((1, 2 * DN_HEADS, tm), lambda bi, ti: (bi, 0, ti)),
    ]
    return pl.pallas_call(
        kern,
        grid=(b, s // tm),
        in_specs=[
            pl.BlockSpec((1, tm, d), row),
            pl.BlockSpec((1, 1, 3 * d), lambda bi, ti: (bi, 0, 0)),
            pl.BlockSpec((1, d), const),
            pl.BlockSpec((1, d, n_tok), this_layer, pipeline_mode=pl.Buffered(1)),
            pl.BlockSpec((1, MB_WIDTH, d), this_layer, pipeline_mode=pl.Buffered(1)),
            pl.BlockSpec((1, 2 * DN_HEADS, d), this_layer, pipeline_mode=pl.Buffered(1)),
            pl.BlockSpec((DN_CONV, 3 * DN_WIDTH), const),
        ],
        out_specs=out_specs,
        out_shape=out_shape,
        scratch_shapes=[
            pltpu.VMEM((3, CONV_HALO + tm, DN_WIDTH), F32),
            pltpu.VMEM((3, CONV_HALO, DN_WIDTH), F32),
        ],
        compiler_params=pltpu.CompilerParams(
            dimension_semantics=("parallel", "arbitrary"),
            vmem_limit_bytes=V7X_VMEM_LIMIT_BYTES),
        name="in_projection",
    )(x, mod, g_pre, w_tok, w_vt, w_bat, conv_w)


def _dn_kernel(qkv_ref, ba_ref, bat_ref, z_ref, pcol_ref, prow_ref, gnorm_ref, o_ref, state_ref):
    t = pl.program_id(1)
    nbat, tt = qkv_ref.shape[0], qkv_ref.shape[1]
    n_chunks = tt // DN_CHUNK
    dk = DN_HEAD_DIM
    cs = DN_CHUNK

    @pl.when(t == 0)
    def _():
        state_ref[...] = jnp.zeros(state_ref.shape, F32)

    ri = lax.broadcasted_iota(jnp.int32, (tt, tt), 0)
    ci = lax.broadcasted_iota(jnp.int32, (tt, tt), 1)
    same_chunk = (ri // DN_CHUNK) == (ci // DN_CHUNK)
    lower = (same_chunk & (ri >= ci)).astype(BF16)
    upper = (same_chunk & (ri <= ci)).astype(BF16)

    beta_all, gc_row, eg, eg_to_last, eg_last = [], [], [], [], []
    for bi in range(nbat):
        ba = ba_ref[bi]
        beta_all.append(jax.nn.sigmoid(ba))
        g_col = -jnp.exp(pcol_ref[0:1, :]) * jax.nn.softplus(ba + pcol_ref[1:2, :])
        g_row = -jnp.exp(prow_ref[:, 0:1]) * jax.nn.softplus(bat_ref[bi] + prow_ref[:, 1:2])
        gch, gcl = _split_bf16(g_col)
        gc_col = _dot(lower, gch) + _dot(lower, gcl)
        grh, grl = _split_bf16(g_row)
        gc_row.append(_dot(grh, upper) + _dot(grl, upper))
        eg.append(jnp.exp(gc_col))
        g_last = [gc_col[(c + 1) * DN_CHUNK - 1:(c + 1) * DN_CHUNK, :] for c in range(n_chunks)]
        eg_to_last.append(jnp.concatenate(
            [jnp.exp(g_last[c] - gc_col[c * DN_CHUNK:(c + 1) * DN_CHUNK, :]) for c in range(n_chunks)], axis=0))
        eg_last.append([jnp.exp(g_last[c]) for c in range(n_chunks)])

    col_chunk = lax.broadcasted_iota(jnp.int32, (dk, tt), 1) // cs

    wr = lax.broadcasted_iota(jnp.int32, (cs, tt), 0)
    wl = lax.broadcasted_iota(jnp.int32, (cs, tt), 1)
    wj = wl % cs
    wc = wl // cs
    causal_w = wr >= wj
    strict_w = wr > wj
    same_chunk_b = same_chunk.astype(BF16)

    def fold(x):
        acc = None
        for c in range(n_chunks):
            part = jnp.where(wc == c, x[c * cs:(c + 1) * cs, :], 0.0)
            acc = part if acc is None else acc + part
        return acc

    def block_diag(xw):
        return jnp.where(same_chunk, jnp.concatenate([xw] * n_chunks, axis=0), jnp.zeros((), xw.dtype))

    def blocks_w(n):
        return (wr // n) == (wj // n)

    heads = range(nbat * DN_HEADS)
    bat = [u // DN_HEADS for u in heads]
    hd = [u % DN_HEADS for u in heads]
    gl = [DN_HEADS + hd[h] for h in heads]
    q_b = [qkv_ref[bat[h], :, hd[h] * dk:(hd[h] + 1) * dk] for h in heads]
    kn_b = [qkv_ref[bat[h], :, DN_WIDTH + hd[h] * dk:DN_WIDTH + (hd[h] + 1) * dk] for h in heads]
    qn = [q_b[h].astype(F32) for h in heads]
    kn = [kn_b[h].astype(F32) for h in heads]
    beta = [beta_all[bat[h]][:, hd[h]:hd[h] + 1] for h in heads]
    kb = [kn[h] * beta[h] for h in heads]
    vb = [qkv_ref[bat[h], :, 2 * DN_WIDTH + hd[h] * dk:2 * DN_WIDTH + (hd[h] + 1) * dk].astype(F32) * beta[h]
          for h in heads]
    scores = [_dot_nt(jnp.concatenate([kb[h].astype(BF16), q_b[h]], axis=0), kn_b[h]) for h in heads]
    g_rows = [gc_row[bat[h]][gl[h]:gl[h] + 1, :] for h in heads]
    gcol_w = []
    for h in heads:
        rh, rl = _split_bf16(jnp.where(wj == wr, g_rows[h], 0.0))
        gcol_w.append(_dot(rh, same_chunk_b) + _dot(rl, same_chunk_b))

    a, qk = [], []
    for h in heads:
        diff = gcol_w[h] - g_rows[h]
        decay = jnp.where(causal_w, jnp.exp(jnp.where(causal_w, diff, 0.0)), 0.0)
        a.append(jnp.where(strict_w, fold(scores[h][0:tt]) * decay, 0.0))
        qk.append(fold(scores[h][tt:2 * tt]) * decay)

    p = [(wr == wj).astype(F32) - jnp.where(blocks_w(2), a[h], 0.0) for h in heads]
    n = 2
    while n < cs:
        lower_left = blocks_w(2 * n) & jnp.logical_not(blocks_w(n))
        p_b = [p[h].astype(BF16) for h in heads]
        mid = [_dot(jnp.where(lower_left, a[h], 0.0).astype(BF16), block_diag(p_b[h])) for h in heads]
        upd = [_dot(p_b[h], block_diag(mid[h].astype(BF16))) for h in heads]
        p = [p[h] - upd[h] for h in heads]
        n *= 2
    p = [block_diag(p[h].astype(BF16)) for h in heads]

    eg_h = [eg[bat[h]][:, gl[h]:gl[h] + 1] for h in heads]
    uw = [_dot(p[h], jnp.concatenate([vb[h].astype(BF16), (kb[h] * eg_h[h]).astype(BF16)], axis=1))
          for h in heads]
    u = [uw[h][:, 0:dk] for h in heads]
    w_b = [uw[h][:, dk:2 * dk].astype(BF16) for h in heads]
    qg_b = [(qn[h] * eg_h[h]).astype(BF16) for h in heads]
    kdt = [(kn[h] * eg_to_last[bat[h]][:, gl[h]:gl[h] + 1]).T for h in heads]

    state = [state_ref[h] for h in heads]
    v_rows = [[] for _ in heads]
    o_rows = [[] for _ in heads]
    for c in range(n_chunks):
        rows = slice(c * DN_CHUNK, (c + 1) * DN_CHUNK)
        r2 = [_dot(jnp.concatenate([w_b[h][rows], qg_b[h][rows]], axis=0), state[h].astype(BF16)) for h in heads]
        lhs3 = []
        for h in heads:
            kdt_c = jnp.where(col_chunk == c, kdt[h], 0.0).astype(BF16)
            qk_c = jnp.where(wc == c, qk[h], 0.0).astype(BF16)
            lhs3.append(jnp.concatenate([qk_c, kdt_c], axis=0))
        r3 = []
        for h in heads:
            v_rows[h].append((u[h][rows] - r2[h][0:DN_CHUNK]).astype(BF16))
            v_blk = jnp.concatenate(
                v_rows[h] + [jnp.zeros(((n_chunks - 1 - c) * DN_CHUNK, dk), BF16)] * (c < n_chunks - 1), axis=0)
            r3.append(_dot(lhs3[h], v_blk))
        for h in heads:
            o_rows[h].append(r2[h][DN_CHUNK:2 * DN_CHUNK] + r3[h][0:DN_CHUNK])
            state[h] = state[h] * eg_last[bat[h]][c][:, gl[h]:gl[h] + 1] + r3[h][DN_CHUNK:DN_CHUNK + dk]

    for h in heads:
        state_ref[h] = state[h]
        o = jnp.concatenate(o_rows[h], axis=0)
        on = o * lax.rsqrt(jnp.mean(o * o, axis=-1, keepdims=True) + NORM_EPS) * gnorm_ref[...]
        zh = z_ref[bat[h], :, hd[h] * dk:(hd[h] + 1) * dk].astype(F32)
        o_ref[bat[h], :, hd[h] * dk:(hd[h] + 1) * dk] = (on * _silu(zh)).astype(o_ref.dtype)


def _deltanet(qkv, ba, bat, z, p_col, p_row, g_norm):
    b, s, w3 = qkv.shape
    tt = DN_TILE
    nbat = DN_BATCH if b % DN_BATCH == 0 else 1
    row = lambda bi, ti: (bi, ti, 0)
    const = lambda bi, ti: (0, 0)
    return pl.pallas_call(
        _dn_kernel,
        grid=(b // nbat, s // tt),
        in_specs=[
            pl.BlockSpec((nbat, tt, w3), row),
            pl.BlockSpec((nbat, tt, LANES), row),
            pl.BlockSpec((nbat, 2 * DN_HEADS, tt), lambda bi, ti: (bi, 0, ti)),
            pl.BlockSpec((nbat, tt, DN_WIDTH), row),
            pl.BlockSpec((2, LANES), const),
            pl.BlockSpec((2 * DN_HEADS, 2), const),
            pl.BlockSpec((1, DN_HEAD_DIM), const),
        ],
        out_specs=pl.BlockSpec((nbat, tt, DN_WIDTH), row),
        out_shape=jax.ShapeDtypeStruct((b, s, DN_WIDTH), BF16),
        scratch_shapes=[
            pltpu.VMEM((nbat * DN_HEADS, DN_HEAD_DIM, DN_HEAD_DIM), F32),
        ],
        compiler_params=pltpu.CompilerParams(
            dimension_semantics=("parallel", "arbitrary"),
            vmem_limit_bytes=V7X_VMEM_LIMIT_BYTES),
        name="deltanet",
    )(qkv, ba, bat, z, p_col, p_row, g_norm)


def _moba_kernel(q_ref, k_ref, vt_ref, z_ref, o_ref, kmean_ref, kaug_ref, qaug_ref, s_ref):
    a = pl.program_id(2)
    i = a * Q_BLOCKS
    nb = vt_ref.shape[1]
    blk = MB_BLOCK
    qw = Q_BLOCKS * blk
    dh = MB_HEAD_DIM
    heads = LANES // dh
    lane = lax.broadcasted_iota(jnp.int32, (1, LANES), 1)
    in_head = [(lane >= h * dh) & (lane < (h + 1) * dh) for h in range(heads)]
    bias_lane0 = [((h + 1) % heads) * dh for h in range(heads)]

    @pl.when(a == 0)
    def _():
        for j in range(nb):
            kj = k_ref[0, j * blk:(j + 1) * blk, :]
            kmean_ref[j:j + 1, :] = jnp.mean(kj.astype(F32), axis=0, keepdims=True)
            for h in range(heads):
                onehot = (lane == bias_lane0[h] + j).astype(BF16)
                kaug_ref[h, j * blk:(j + 1) * blk, :] = jnp.where(in_head[h], kj, onehot)
        for h in range(heads):
            kaug_ref[h, nb * blk:(nb + 1) * blk, :] = jnp.broadcast_to(
                (lane == bias_lane0[h] + nb).astype(BF16), (blk, LANES))
        km = kmean_ref[...]
        pieces = []
        for h in range(heads):
            rest = jnp.where(in_head[h], km, 0.0)
            for _ in range(3):
                part = rest.astype(BF16)
                pieces.append(part)
                rest = rest - part.astype(F32)
        km_parts = jnp.concatenate(pieces, axis=0)
        jidx = lax.broadcasted_iota(jnp.int32, (nb, blk), 0)
        gates = []
        for iq in range(nb):
            g_all = _dot_nt(km_parts, q_ref[0, iq * blk:(iq + 1) * blk, :])
            gates.append([g_all[(3 * h) * nb:(3 * h + 1) * nb] + g_all[(3 * h + 1) * nb:(3 * h + 2) * nb]
                          + g_all[(3 * h + 2) * nb:(3 * h + 3) * nb] for h in range(heads)])
        pad_rows = dh - nb
        for iq in range(nb):
            bias = []
            for h in range(heads):
                gate = jnp.where(jidx < iq, gates[iq][h], -jnp.inf)
                rank = jnp.zeros((nb, blk), jnp.int32)
                for jp in range(iq):
                    other = gate[jp:jp + 1, :]
                    beats = (other > gate) | ((other == gate) & (jp < jidx))
                    rank = rank + beats.astype(jnp.int32)
                keep = (rank < MB_TOPK) & (jidx < iq)
                bias.append(jnp.where(keep, 0.0, MASK_VALUE))
            bias_rows = []
            for h in range(heads):
                other = (h + 1) % heads
                bias_rows += [bias[other], jnp.full((1, blk), MASK_VALUE, F32), jnp.zeros((pad_rows - 1, blk), F32)]
            bias_t = jnp.concatenate(bias_rows, axis=0).T.astype(BF16)
            q_blk = q_ref[0, iq * blk:(iq + 1) * blk, :]
            for h in range(heads):
                qaug_ref[h, iq * blk:(iq + 1) * blk, :] = jnp.where(in_head[h], q_blk, bias_t)

    q_aug = [qaug_ref[h, pl.ds(pl.multiple_of(a * qw, qw), qw), :] for h in range(heads)]
    ones_rows = jnp.ones((BF16_SUBLANES, blk), BF16)

    def v_ext(jb, h):
        return jnp.concatenate([vt_ref[0, jb, h * dh:(h + 1) * dh, :], ones_rows], axis=0)

    def score_dots(t):
        tiles = []
        for h in range(heads):
            for g in range(KV_GROUP):
                j = t * KV_GROUP + g
                kb = jnp.where(j < i, j, nb)
                tiles.append(_dot_nt(kaug_ref[h, pl.ds(pl.multiple_of(kb * blk, blk), blk), :], q_aug[h]))
        return tiles

    def stash(tiles):
        maxima = []
        for h in range(heads):
            cm = None
            for g in range(KV_GROUP):
                st = tiles[h * KV_GROUP + g]
                s_ref[h * KV_GROUP + g] = st
                gm = jnp.max(st, axis=0, keepdims=True)
                cm = gm if cm is None else jnp.maximum(cm, gm)
            maxima.append(cm)
        return maxima

    q_row = lax.broadcasted_iota(jnp.int32, (qw, 1), 0)
    own_tiles = []
    for h in range(heads):
        plain = jnp.where(in_head[h], q_aug[h], jnp.zeros_like(q_aug[h]))
        for d in range(Q_BLOCKS):
            kd = kaug_ref[h, pl.ds(pl.multiple_of((i + d) * blk, blk), blk), :]
            q_own = jnp.where(q_row < (d + 1) * blk, plain, q_aug[h])
            own_tiles.append(_dot_nt(kd, q_own[d * blk:qw]))
    tiles0 = score_dots(0)
    ms, ps = [], []
    for h in range(heads):
        st = []
        m = None
        for d in range(Q_BLOCKS):
            width = qw - d * blk
            key_pos = lax.broadcasted_iota(jnp.int32, (blk, width), 0)
            rel = lax.broadcasted_iota(jnp.int32, (blk, width), 1)
            visible = (rel >= blk) | (key_pos <= rel)
            st.append(jnp.where(visible, own_tiles[h * Q_BLOCKS + d], MASK_VALUE))
            md = jnp.max(st[d], axis=0, keepdims=True)
            if d > 0:
                md = jnp.concatenate([jnp.full((1, d * blk), MASK_VALUE, F32), md], axis=1)
            m = md if m is None else jnp.maximum(m, md)
        ms.append(m)
        ps.append([jnp.exp2(st[d] - m[:, d * blk:qw]).astype(BF16) for d in range(Q_BLOCKS)])
    cms = stash(tiles0)
    carry = []
    for h in range(heads):
        acc = None
        for d in range(Q_BLOCKS):
            term = _dot(v_ext(i + d, h), ps[h][d])
            if d > 0:
                term = jnp.concatenate([jnp.zeros((dh + BF16_SUBLANES, d * blk), F32), term], axis=1)
            acc = term if acc is None else acc + term
        carry += [ms[h], acc, cms[h]]

    def body(t, carry):
        tiles_next = score_dots(t + 1)
        new_m, alphas, ps = [], [], []
        for h in range(heads):
            m, _, cm = carry[3 * h:3 * h + 3]
            m_new = jnp.maximum(m, cm)
            new_m.append(m_new)
            alphas.append(jnp.exp2(m - m_new))
            ps.append([jnp.exp2(s_ref[h * KV_GROUP + g] - m_new).astype(BF16) for g in range(KV_GROUP)])
        cm_next = stash(tiles_next)
        out = []
        for h in range(heads):
            pv = None
            for g in range(KV_GROUP):
                vb = jnp.minimum(t * KV_GROUP + g, nb - 1)
                term = _dot(v_ext(vb, h), ps[h][g])
                pv = term if pv is None else pv + term
            out += [new_m[h], alphas[h] * carry[3 * h + 1] + pv, cm_next[h]]
        return tuple(out)

    carry = lax.fori_loop(0, (i + KV_GROUP - 1) // KV_GROUP, body, tuple(carry))
    ot = jnp.concatenate([carry[3 * h + 1][0:dh] / carry[3 * h + 1][dh:dh + 1] for h in range(heads)], axis=0)
    o = ot.T
    o_ref[0] = (o * _silu(z_ref[0].astype(F32))).astype(o_ref.dtype)


def _moba(qk, vt, z):
    b, s, _ = qk.shape
    nb = s // MB_BLOCK
    pairs = MB_WIDTH // LANES
    heads = LANES // MB_HEAD_DIM
    assert nb + 1 <= MB_HEAD_DIM
    assert nb % Q_BLOCKS == 0
    q_rows = Q_BLOCKS * MB_BLOCK
    return pl.pallas_call(
        _moba_kernel,
        grid=(b, pairs, nb // Q_BLOCKS),
        in_specs=[
            pl.BlockSpec((1, s, LANES), lambda bi, hp, i: (bi, 0, hp)),
            pl.BlockSpec((1, s, LANES), lambda bi, hp, i: (bi, 0, pairs + hp)),
            pl.BlockSpec((1, nb, LANES, MB_BLOCK), lambda bi, hp, i: (bi, 0, hp, 0)),
            pl.BlockSpec((1, q_rows, LANES), lambda bi, hp, i: (bi, i, hp)),
        ],
        out_specs=pl.BlockSpec((1, q_rows, LANES), lambda bi, hp, i: (bi, i, hp)),
        out_shape=jax.ShapeDtypeStruct((b, s, MB_WIDTH), BF16),
        scratch_shapes=[
            pltpu.VMEM((nb, LANES), F32),
            pltpu.VMEM((heads, s + MB_BLOCK, LANES), BF16),
            pltpu.VMEM((heads, s, LANES), BF16),
            pltpu.VMEM((heads * KV_GROUP, MB_BLOCK, q_rows), F32),
        ],
        compiler_params=pltpu.CompilerParams(
            dimension_semantics=("parallel", "parallel", "arbitrary"),
            vmem_limit_bytes=V7X_VMEM_LIMIT_BYTES),
        name="moba_attention",
    )(qk, qk, vt, z)


def _outproj_kernel(x_ref, odn_ref, omb_ref, merge_ref, mod_ref, gpost_ref,
                    wdn_ref, wmb_ref, wout_ref, o_ref, *, d_model):
    y_dn = _dot(odn_ref[0], wdn_ref[...])
    y_mb = _dot(omb_ref[0], wmb_ref[...])
    gates = jax.nn.sigmoid(merge_ref[0].astype(F32))
    mixed_in = gates[:, 0:d_model] * y_dn + gates[:, d_model:2 * d_model] * y_mb
    mixed = _dot(mixed_in.astype(BF16), wout_ref[...])
    y = mixed * lax.rsqrt(jnp.mean(mixed * mixed, axis=-1, keepdims=True) + NORM_EPS) * gpost_ref[...]
    gate = mod_ref[0, :, 2 * d_model:3 * d_model]
    o_ref[0] = x_ref[0] + gate * y


def _out_projection(x, o_dn, o_mb, merge, mod, g_post, w_dn, w_mb, w_out):
    b, s, d = x.shape
    tm = OUT_TILE if s % OUT_TILE == 0 else ROW_TILE
    row = lambda bi, ti: (bi, ti, 0)
    const = lambda bi, ti: (0, 0)
    kern = functools.partial(_outproj_kernel, d_model=d)
    return pl.pallas_call(
        kern,
        grid=(b, s // tm),
        in_specs=[
            pl.BlockSpec((1, tm, d), row),
            pl.BlockSpec((1, tm, DN_WIDTH), row),
            pl.BlockSpec((1, tm, MB_WIDTH), row),
            pl.BlockSpec((1, tm, 2 * d), row),
            pl.BlockSpec((1, 1, 3 * d), lambda bi, ti: (bi, 0, 0)),
            pl.BlockSpec((1, d), const),
            pl.BlockSpec((DN_WIDTH, d), const),
            pl.BlockSpec((MB_WIDTH, d), const),
            pl.BlockSpec((d, d), const),
        ],
        out_specs=pl.BlockSpec((1, tm, d), row),
        out_shape=jax.ShapeDtypeStruct((b, s, d), F32),
        compiler_params=pltpu.CompilerParams(
            dimension_semantics=("parallel", "parallel"),
            vmem_limit_bytes=V7X_VMEM_LIMIT_BYTES),
        name="out_projection",
    )(x, o_dn, o_mb, merge, mod, g_post, w_dn, w_mb, w_out)


def _split_w_in(w_in, d_model):
    sizes = (3 * DN_WIDTH, DN_WIDTH, DN_HEADS, DN_HEADS, 3 * MB_WIDTH, MB_WIDTH, 2 * d_model)
    cuts = [0]
    for n in sizes:
        cuts.append(cuts[-1] + n)
    w_qkv_dn, w_z_dn, w_beta, w_a, w_qkv_mb, w_z_mb, w_merge = (
        w_in[:, :, cuts[i]:cuts[i + 1]] for i in range(len(sizes)))
    w_ba = jnp.concatenate([w_beta, w_a], axis=2)
    w_ba_pad = jnp.pad(w_ba, ((0, 0), (0, 0), (0, LANES - 2 * DN_HEADS)))
    w_q_mb = w_qkv_mb[:, :, 0:MB_WIDTH] * (MB_HEAD_DIM ** -0.5 * LOG2_E)
    w_k_mb = w_qkv_mb[:, :, MB_WIDTH:2 * MB_WIDTH]
    w_tok = jnp.concatenate(
        [w_qkv_dn, w_z_dn, w_q_mb, w_k_mb, w_z_mb, w_merge, w_ba_pad], axis=2).astype(BF16)
    w_vt = jnp.swapaxes(w_qkv_mb[:, :, 2 * MB_WIDTH:3 * MB_WIDTH], 1, 2).astype(BF16)
    w_bat = jnp.swapaxes(w_ba, 1, 2).astype(BF16)
    return w_tok, w_vt, w_bat


def _layer(x, mod, g_pre, g_post, w_in_groups, layer, conv_w, a_log, dt_bias, dn_norm_g,
           w_proj_dn, w_proj_mb, w_out):
    b, s, d = x.shape
    w_tok, w_vt, w_bat = w_in_groups
    qkv_dn, z_dn, qk_mb, z_mb, merge, ba, vt, bat = _in_projection(
        x, mod, g_pre.reshape(1, d), w_tok, w_vt, w_bat, conv_w.astype(F32), layer)

    zeros_h = jnp.zeros((DN_HEADS,), F32)
    a_vec = jnp.concatenate([zeros_h, a_log.astype(F32)])
    dt_vec = jnp.concatenate([zeros_h, dt_bias.astype(F32)])
    p_col = jnp.pad(jnp.stack([a_vec, dt_vec]), ((0, 0), (0, LANES - 2 * DN_HEADS)))
    p_row = jnp.stack([a_vec, dt_vec], axis=1)
    o_dn = _deltanet(qkv_dn, ba, bat, z_dn, p_col, p_row, dn_norm_g.reshape(1, DN_HEAD_DIM).astype(F32))
    o_mb = _moba(qk_mb, vt, z_mb)
    return _out_projection(x, o_dn, o_mb, merge, mod, g_post.reshape(1, d),
                           w_proj_dn.astype(BF16), w_proj_mb.astype(BF16), w_out.astype(BF16))


def kernel(x, c, w_ada, b_ada, g_pre, g_post, w_in, conv_w, a_log, dt_bias, dn_norm_g, w_proj_dn, w_proj_mb, w_out):
    depth = w_ada.shape[0]
    b, s, d = x.shape
    assert s % ROW_TILE == 0 and s % MB_BLOCK == 0 and s % DN_TILE == 0
    mod_all = _modulation(c, w_ada, b_ada)
    w_in_groups = _split_w_in(w_in, d)
    for l in range(depth):
        mod = mod_all[l].reshape(b, 1, 3 * d)
        x = _layer(x, mod, g_pre[l], g_post[l], w_in_groups, l, conv_w[l], a_log[l], dt_bias[l],
                   dn_norm_g[l], w_proj_dn[l], w_proj_mb[l], w_out[l])
    return x
```
